```python
import jax, jax.numpy as jnp
from jax import lax
import numpy as np

D_MODEL = 1024
BATCH = 16
SEQ = 256
DEPTH = 4
DEC_BATCH = 8
DEC_SEQ = 1024
PAST_LEN = 512

GRID_W = 64
N_MIXERS = 2
N_DELTA = (DEPTH + 1) // 2
N_ATTN = DEPTH // 2
DN_DK = 128
DN_DV = 128
DN_HEADS = D_MODEL // DN_DK
DN_CONV = 3
DN_CHUNK = 64
DN_KD = DN_HEADS * DN_DK
DN_VD = DN_HEADS * DN_DV
DN_QKV = 2 * DN_KD + DN_VD
DN_PROJ = DN_QKV + DN_VD + 4 * DN_HEADS
AT_HEAD_DIM = 64
AT_HEADS = D_MODEL // AT_HEAD_DIM
AT_KV_HEADS = AT_HEADS // 4
AT_GROUPS = AT_HEADS // AT_KV_HEADS
AT_WINDOW = 128
AT_QBLK = 128
AT_PROJ = (AT_HEADS + 2 * AT_KV_HEADS) * AT_HEAD_DIM
ROPE_BASE = 10000.0
FFN_DIM = ((8 * D_MODEL // 3 + 127) // 128) * 128
FFN_CONV = 3
EPS = 1e-6

kernel_name = "hybrid_deltanet_swa_diffusion_step"

F32 = jnp.float32


def rmsnorm(x, g):
    x32 = x.astype(F32)
    y = x32 * lax.rsqrt(jnp.mean(x32 * x32, axis=-1, keepdims=True) + EPS)
    return (y * g.astype(F32)).astype(x.dtype)


def l2norm(x):
    x32 = x.astype(F32)
    return x32 * lax.rsqrt(jnp.sum(x32 * x32, axis=-1, keepdims=True) + EPS)


def dwconv_centred(x, w, b=None):
    K = w.shape[0]
    pad = K // 2
    T = x.shape[1]
    xp = jnp.pad(x, ((0, 0), (pad, pad), (0, 0)))
    y = sum(xp[:, k:k + T] * w[k] for k in range(K))
    if b is not None:
        y = y + b
    return y


def modulation(cvec, w_ada, b_ada):
    m = (jax.nn.silu(cvec) @ w_ada + b_ada)[:, None, :]
    return jnp.split(m, 6, axis=-1)


def pre(x, g, shift, scale):
    return rmsnorm(x, g) * (1 + scale) + shift


def chunk_gated_delta(q, k, v, g, beta, s0):
    B, T, H, DK = q.shape
    DV = v.shape[-1]
    C = DN_CHUNK
    n = T // C

    def blocks(a):
        a = jnp.moveaxis(a.astype(F32), 2, 1)
        return a.reshape((a.shape[0], a.shape[1], n, C) + a.shape[3:])

    qc = blocks(q) * (DK ** -0.5)
    kc = blocks(k)
    vc = blocks(v)
    bc = blocks(beta)
    gc = jnp.cumsum(blocks(g), axis=-1)
    idx = jnp.arange(C)
    lower = idx[:, None] >= idx[None, :]
    strict = idx[:, None] > idx[None, :]
    decay = jnp.exp(jnp.where(lower, gc[..., :, None] - gc[..., None, :], -jnp.inf))
    kk = jnp.einsum('bhnik,bhnjk->bhnij', kc * bc[..., None], kc) * decay
    a_mat = jnp.where(strict, kk, 0.0) + jnp.eye(C, dtype=F32)
    rhs = jnp.concatenate([vc * bc[..., None], kc * (bc * jnp.exp(gc))[..., None]], axis=-1)
    sol = lax.linalg.triangular_solve(a_mat, rhs, left_side=True, lower=True, unit_diagonal=True)
    u, w = sol[..., :DV], sol[..., DV:]
    qk = jnp.where(lower, jnp.einsum('bhnik,bhnjk->bhnij', qc, kc) * decay, 0.0)

    def step(S, xs):
        q_i, k_i, u_i, w_i, g_i, qk_i = xs
        v_new = u_i - jnp.einsum('bhck,bhkv->bhcv', w_i, S)
        o_i = (jnp.einsum('bhck,bhkv->bhcv', q_i * jnp.exp(g_i)[..., None], S)
               + jnp.einsum('bhij,bhjv->bhiv', qk_i, v_new))
        g_last = g_i[..., -1]
        S = (S * jnp.exp(g_last)[..., None, None]
             + jnp.einsum('bhck,bhcv->bhkv', k_i * jnp.exp(g_last[..., None] - g_i)[..., None], v_new))
        return S, o_i

    xs = tuple(jnp.moveaxis(a, 2, 0) for a in (qc, kc, u, w, gc, qk))
    s_fin, o = lax.scan(step, s0.astype(F32), xs)
    o = jnp.transpose(o, (1, 0, 3, 2, 4)).reshape(B, T, H, DV)
    return o, s_fin


def gated_delta_mixer(h, s0, w_in, conv_w, a_log, dt_bias, onorm_g, w_out):
    B, T, _ = h.shape
    proj = h @ w_in
    qkv = jax.nn.silu(dwconv_centred(proj[..., :DN_QKV], conv_w))
    q = l2norm(qkv[..., :DN_KD].reshape(B, T, DN_HEADS, DN_DK))
    k = l2norm(qkv[..., DN_KD:2 * DN_KD].reshape(B, T, DN_HEADS, DN_DK))
    v = qkv[..., 2 * DN_KD:].reshape(B, T, DN_HEADS, DN_DV)
    z = proj[..., DN_QKV:DN_QKV + DN_VD].reshape(B, T, DN_HEADS, DN_DV)
    ab = proj[..., DN_QKV + DN_VD:].astype(F32).reshape(B, T, 2, 2, DN_HEADS)
    g = -jnp.exp(a_log.astype(F32)) * jax.nn.softplus(ab[:, :, 0] + dt_bias.astype(F32))
    beta = jax.nn.sigmoid(ab[:, :, 1])
    flip = lambda a: jnp.flip(a, axis=1)
    o_f, s_f = chunk_gated_delta(q, k, v, g[:, :, 0], beta[:, :, 0], s0[:, 0])
    o_b, s_b = chunk_gated_delta(flip(q), flip(k), flip(v), flip(g[:, :, 1]), flip(beta[:, :, 1]), s0[:, 1])
    o = o_f + flip(o_b)
    o = rmsnorm(o, onorm_g) * jax.nn.silu(z.astype(F32))
    y = o.reshape(B, T, DN_VD).astype(h.dtype) @ w_out
    return y, jnp.stack([s_f, s_b], axis=1).astype(h.dtype)


def attn_proj(h, w_qkv):
    B, T, _ = h.shape
    p = h @ w_qkv
    qd = AT_HEADS * AT_HEAD_DIM
    kd = AT_KV_HEADS * AT_HEAD_DIM
    q = p[..., :qd].reshape(B, T, AT_KV_HEADS, AT_GROUPS, AT_HEAD_DIM)
    k = p[..., qd:qd + kd].reshape(B, T, AT_KV_HEADS, AT_HEAD_DIM)
    v = p[..., qd + kd:].reshape(B, T, AT_KV_HEADS, AT_HEAD_DIM)
    return q, k, v


def rope2d(x):
    T = x.shape[1]
    rows = T // GRID_W
    pos = jnp.arange(rows * GRID_W)
    row = (pos // GRID_W).astype(F32)
    col = (pos % GRID_W).astype(F32)
    half = AT_HEAD_DIM // 2
    inv = jnp.power(ROPE_BASE, -jnp.arange(0, half, 2, dtype=F32) / half)

    def rot(xh, p):
        ang = (p[:, None] * inv).reshape((1, T) + (1,) * (x.ndim - 3) + (-1,))
        cos, sin = jnp.cos(ang), jnp.sin(ang)
        x1, x2 = jnp.split(xh.astype(F32), 2, axis=-1)
        return jnp.concatenate([x1 * cos - x2 * sin, x1 * sin + x2 * cos], axis=-1)

    return jnp.concatenate([rot(x[..., :half], row), rot(x[..., half:], col)], axis=-1).astype(x.dtype)


def softmax_with_sink(s, sink):
    sk = jnp.broadcast_to(sink.astype(F32).reshape(1, AT_KV_HEADS, AT_GROUPS, 1, 1), s.shape[:-1] + (1,))
    return jax.nn.softmax(jnp.concatenate([s, sk], axis=-1), axis=-1)[..., :-1]


def context_attention(q, k, v, sink):
    B, T = q.shape[:2]
    n = T // AT_QBLK
    scale = AT_HEAD_DIM ** -0.5
    qb = jnp.moveaxis(q.reshape(B, n, AT_QBLK, AT_KV_HEADS, AT_GROUPS, AT_HEAD_DIM), 1, 0)

    def blk(q_i):
        s = jnp.einsum('bqkgd,bskd->bkgqs', q_i, k, preferred_element_type=F32) * scale
        p = softmax_with_sink(s, sink)
        return jnp.einsum('bkgqs,bskd->bqkgd', p.astype(v.dtype), v)

    o = lax.map(blk, qb)
    return jnp.moveaxis(o, 0, 1).reshape(B, T, AT_HEADS * AT_HEAD_DIM)


def latent_attention(q, k, v, k_ctx, v_ctx, sink):
    B, T = q.shape[:2]
    n = T // AT_QBLK
    W = AT_WINDOW
    band = AT_QBLK + 2 * W
    scale = AT_HEAD_DIM ** -0.5
    kp = jnp.pad(k, ((0, 0), (W, W), (0, 0), (0, 0)))
    vp = jnp.pad(v, ((0, 0), (W, W), (0, 0), (0, 0)))
    qb = jnp.moveaxis(q.reshape(B, n, AT_QBLK, AT_KV_HEADS, AT_GROUPS, AT_HEAD_DIM), 1, 0)
    qi_idx = jnp.arange(AT_QBLK)
    kj_idx = jnp.arange(band)

    def blk(xs):
        q_i, b = xs
        start = b * AT_QBLK
        kb = lax.dynamic_slice_in_dim(kp, start, band, axis=1)
        vb = lax.dynamic_slice_in_dim(vp, start, band, axis=1)
        qpos = start + qi_idx
        kpos = start - W + kj_idx
        valid = (jnp.abs(qpos[:, None] - kpos[None, :]) <= W) & (kpos[None, :] >= 0) & (kpos[None, :] < T)
        s_loc = jnp.einsum('bqkgd,bskd->bkgqs', q_i, kb, preferred_element_type=F32) * scale
        s_loc = jnp.where(valid, s_loc, -jnp.inf)
        s_ctx = jnp.einsum('bqkgd,bskd->bkgqs', q_i, k_ctx, preferred_element_type=F32) * scale
        p = softmax_with_sink(jnp.concatenate([s_loc, s_ctx], axis=-1), sink).astype(v.dtype)
        return (jnp.einsum('bkgqs,bskd->bqkgd', p[..., :band], vb)
                + jnp.einsum('bkgqs,bskd->bqkgd', p[..., band:], v_ctx))

    o = lax.map(blk, (qb, jnp.arange(n)))
    return jnp.moveaxis(o, 0, 1).reshape(B, T, AT_HEADS * AT_HEAD_DIM)


def conv_ffn(h, w_up, conv_w, conv_b, w_down):
    u = dwconv_centred(h @ w_up, conv_w, conv_b)
    a, b = jnp.split(u, 2, axis=-1)
    return (jax.nn.silu(a) * b) @ w_down


def setup_inputs(seed: int = 0) -> dict:
    key = jax.random.key(seed)
    ks = jax.random.split(key, 24)
    nrm = lambda k, s, sc: jax.random.normal(k, s, F32) * sc
    D = D_MODEL
    return {
        "x_prompt": nrm(ks[0], (BATCH, SEQ, D), 1.0),
        "x_sample": nrm(ks[1], (DEC_BATCH, DEC_SEQ, D), 1.0),
        "state_delta": nrm(ks[2], (DEC_BATCH, N_DELTA, 2, DN_HEADS, DN_DK, DN_DV), 0.2),
        "cache_k": nrm(ks[3], (DEC_BATCH, N_ATTN, PAST_LEN, AT_KV_HEADS, AT_HEAD_DIM), 1.0),
        "cache_v": nrm(ks[4], (DEC_BATCH, N_ATTN, PAST_LEN, AT_KV_HEADS, AT_HEAD_DIM), 1.0),
        "c": nrm(ks[5], (DEC_BATCH, D), 1.0),
        "c_ctx": nrm(ks[6], (D,), 1.0),
        "w_ada": nrm(ks[7], (DEPTH, D, 6 * D), 0.5 * D ** -0.5),
        "b_ada": nrm(ks[8], (DEPTH, 6 * D), 0.02),
        "norm_g": 1.0 + nrm(ks[9], (DEPTH, 4, D), 0.1),
        "dn_w_in": nrm(ks[10], (N_DELTA, D, DN_PROJ), D ** -0.5),
        "dn_conv_w": nrm(ks[11], (N_DELTA, DN_CONV, DN_QKV), DN_CONV ** -0.5),
        "dn_a_log": jnp.log(jax.random.uniform(ks[12], (N_DELTA, 2, DN_HEADS), F32, 1.0, 16.0)),
        "dn_dt_bias": jax.random.uniform(ks[13], (N_DELTA, 2, DN_HEADS), F32, -4.6, -2.3),
        "dn_onorm_g": 1.0 + nrm(ks[14], (N_DELTA, DN_DV), 0.1),
        "dn_w_out": nrm(ks[15], (N_DELTA, DN_VD, D), DN_VD ** -0.5),
        "at_w_qkv": nrm(ks[16], (N_ATTN, D, AT_PROJ), D ** -0.5),
        "at_sink": nrm(ks[17], (N_ATTN, AT_HEADS), 0.5),
        "at_w_o": nrm(ks[18], (N_ATTN, AT_HEADS * AT_HEAD_DIM, D), (AT_HEADS * AT_HEAD_DIM) ** -0.5),
        "ffn_w_up": nrm(ks[19], (DEPTH, D, 2 * FFN_DIM), D ** -0.5),
        "ffn_conv_w": nrm(ks[20], (DEPTH, FFN_CONV, 2 * FFN_DIM), FFN_CONV ** -0.5),
        "ffn_conv_b": nrm(ks[21], (DEPTH, 2 * FFN_DIM), 0.02),
        "ffn_w_down": nrm(ks[22], (DEPTH, FFN_DIM, D), FFN_DIM ** -0.5),
    }


def reference(x_prompt, x_sample, state_delta, cache_k, cache_v, c, c_ctx,
              w_ada, b_ada, norm_g,
              dn_w_in, dn_conv_w, dn_a_log, dn_dt_bias, dn_onorm_g, dn_w_out,
              at_w_qkv, at_sink, at_w_o,
              ffn_w_up, ffn_conv_w, ffn_conv_b, ffn_w_down):
    xp, xs = x_prompt, x_sample
    st_new, ck_new, cv_new = [], [], []
    for i in range(DEPTH):
        j = i // N_MIXERS
        mp = modulation(c_ctx[None, :], w_ada[i], b_ada[i])
        ms = modulation(c, w_ada[i], b_ada[i])
        hp = pre(xp, norm_g[i, 0], mp[0], mp[1])
        hs = pre(xs, norm_g[i, 0], ms[0], ms[1])
        if i % N_MIXERS == 0:
            s_zero = jnp.zeros((xp.shape[0], 2, DN_HEADS, DN_DK, DN_DV), xp.dtype)
            yp, sp = gated_delta_mixer(hp, s_zero, dn_w_in[j], dn_conv_w[j], dn_a_log[j],
                                       dn_dt_bias[j], dn_onorm_g[j], dn_w_out[j])
            ys, _ = gated_delta_mixer(hs, state_delta[:, j], dn_w_in[j], dn_conv_w[j], dn_a_log[j],
                                      dn_dt_bias[j], dn_onorm_g[j], dn_w_out[j])
            st_new.append(sp)
        else:
            qp, kp, vp = attn_proj(hp, at_w_qkv[j])
            yp = context_attention(qp, kp, vp, at_sink[j]) @ at_w_o[j]
            qs, ks_, vs = attn_proj(hs, at_w_qkv[j])
            ys = latent_attention(rope2d(qs), rope2d(ks_), vs, cache_k[:, j], cache_v[:, j],
                                  at_sink[j]) @ at_w_o[j]
            ck_new.append(kp)
            cv_new.append(vp)
        xp = xp + mp[2] * rmsnorm(yp, norm_g[i, 1])
        xs = xs + ms[2] * rmsnorm(ys, norm_g[i, 1])
        hp = pre(xp, norm_g[i, 2], mp[3], mp[4])
        hs = pre(xs, norm_g[i, 2], ms[3], ms[4])
        fp = conv_ffn(hp, ffn_w_up[i], ffn_conv_w[i], ffn_conv_b[i], ffn_w_down[i])
        fs = conv_ffn(hs, ffn_w_up[i], ffn_conv_w[i], ffn_conv_b[i], ffn_w_down[i])
        xp = xp + mp[5] * rmsnorm(fp, norm_g[i, 3])
        xs = xs + ms[5] * rmsnorm(fs, norm_g[i, 3])
    y_prompt = xp
    y_sample = xs
    state_delta_new = jnp.stack(st_new, axis=1)
    cache_k_new = jnp.stack(ck_new, axis=1)
    cache_v_new = jnp.stack(cv_new, axis=1)
    return (y_prompt, y_sample, state_delta_new, cache_k_new, cache_v_new)
```

```python
import functools

import jax
import jax.numpy as jnp
from jax import lax
from jax.experimental import pallas as pl
from jax.experimental.pallas import tpu as pltpu

F32 = jnp.float32
BF16 = jnp.bfloat16

D = 1024
N_PROMPT, L_PROMPT = 16, 256
N_LATENT, L_LATENT = 8, 1024
PAST = 512
GRID_W = 64
DEPTH = 4
HEADS_DN, DK = 8, 128
CHUNK = 64
DN_MAIN = 4 * D
AT_HEADS, AT_KV, AT_G, HD = 16, 4, 4, 64
AT_Q = AT_HEADS * HD
AT_KVW = AT_KV * HD
WINDOW = 128
QBLK = 128
BAND = QBLK + 2 * WINDOW
FFN = 2816
EPS = 1e-6
ROPE_BASE = 10000.0

ROW_BLK = 1024
TOK_PROMPT = N_PROMPT * L_PROMPT
N_TOK = TOK_PROMPT + N_LATENT * L_LATENT
PROMPT_BLKS = TOK_PROMPT // ROW_BLK
N_BLKS = N_TOK // ROW_BLK
MOD_ROWS = 16
LANES = 128
FFN_TN = 256
DN_TN = 512
AT_TN = 256
VMEM_LIMIT = 56 * 1024 * 1024


def _mod_row(i):
    return jnp.where(i < PROMPT_BLKS, 0, i - (PROMPT_BLKS - 1))


def _mod_spec(layer, chunk):
    return pl.BlockSpec((None, None, 1, D), lambda i, *_: (layer, _mod_row(i), 0, chunk))


def _gain_spec(layer, k):
    return pl.BlockSpec((None, None, 1, D), lambda i, *_: (layer, k, 0, 0))


def _rms(x, g):
    return x * lax.rsqrt(jnp.mean(x * x, axis=-1, keepdims=True) + EPS) * g


def _pre(x, g, shift, scale):
    return _rms(x, g) * (1.0 + scale) + shift


def _silu(x):
    return x * jax.nn.sigmoid(x)


def _seq_edges(i, rows):
    seq_len = jnp.where(i < PROMPT_BLKS, L_PROMPT, L_LATENT)
    pos = lax.broadcasted_iota(jnp.int32, (rows, 1), 0) & (seq_len - 1)
    return pos == 0, pos == seq_len - 1


def _conv3(u, w, first, last):
    rows = u.shape[0]
    prev = jnp.where(first, 0.0, pltpu.roll(u, 1, 0))
    nxt = jnp.where(last, 0.0, pltpu.roll(u, rows - 1, 0))
    return prev * w[0:1] + u * w[1:2] + nxt * w[2:3]


def _mod_kernel(c_ref, w_ref, b_ref, o_ref):
    s = _silu(c_ref[...]).astype(BF16)
    o_ref[...] = jnp.dot(s, w_ref[...].astype(BF16), preferred_element_type=F32) + b_ref[...]


def _modulation(cvecs, w_ada, b_ada):
    tn = 1536
    return pl.pallas_call(
        _mod_kernel,
        grid=(DEPTH, 6 * D // tn),
        in_specs=[
            pl.BlockSpec((MOD_ROWS, D), lambda l, j: (0, 0)),
            pl.BlockSpec((None, D, tn), lambda l, j: (l, 0, j)),
            pl.BlockSpec((None, 1, tn), lambda l, j: (l, 0, j)),
        ],
        out_specs=pl.BlockSpec((None, MOD_ROWS, tn), lambda l, j: (l, 0, j)),
        out_shape=jax.ShapeDtypeStruct((DEPTH, MOD_ROWS, 6 * D), F32),
        compiler_params=pltpu.CompilerParams(
            dimension_semantics=("arbitrary", "arbitrary"), vmem_limit_bytes=VMEM_LIMIT),
        name="modulation",
    )(cvecs, w_ada, b_ada.reshape(DEPTH, 1, 6 * D))


def _ffn_kernel(x_ref, sh_ref, sc_ref, gt_ref, gpre_ref, gpost_ref, wa_ref, wb_ref, cwa_ref, cwb_ref,
                cba_ref, cbb_ref, wd_ref, o_ref, h_scr, acc_scr):
    i, j = pl.program_id(0), pl.program_id(1)

    @pl.when(j == 0)
    def _():
        h_scr[...] = _pre(x_ref[...], gpre_ref[...], sh_ref[...], sc_ref[...]).astype(BF16)
        acc_scr[...] = jnp.zeros_like(acc_scr)

    h = h_scr[...]
    first, last = _seq_edges(i, ROW_BLK)
    a = _conv3(jnp.dot(h, wa_ref[...], preferred_element_type=F32), cwa_ref[...], first, last) + cba_ref[...]
    b = _conv3(jnp.dot(h, wb_ref[...], preferred_element_type=F32), cwb_ref[...], first, last) + cbb_ref[...]
    act = (_silu(a) * b).astype(BF16)
    acc_scr[...] += jnp.dot(act, wd_ref[...], preferred_element_type=F32)

    @pl.when(j == pl.num_programs(1) - 1)
    def _():
        o_ref[...] = x_ref[...] + gt_ref[...] * _rms(acc_scr[...], gpost_ref[...])


def _ffn_layer(x, mod, gains, layer, w_up, conv_w, conv_b, w_down):
    nj = FFN // FFN_TN
    row = pl.BlockSpec((ROW_BLK, D), lambda i, j: (i, 0))
    return pl.pallas_call(
        _ffn_kernel,
        grid=(N_BLKS, nj),
        in_specs=[
            row, _mod_spec(layer, 3), _mod_spec(layer, 4), _mod_spec(layer, 5),
            _gain_spec(layer, 2), _gain_spec(layer, 3),
            pl.BlockSpec((D, FFN_TN), lambda i, j: (0, j)),
            pl.BlockSpec((D, FFN_TN), lambda i, j: (0, nj + j)),
            pl.BlockSpec((3, FFN_TN), lambda i, j: (0, j)),
            pl.BlockSpec((3, FFN_TN), lambda i, j: (0, nj + j)),
            pl.BlockSpec((1, FFN_TN), lambda i, j: (0, j)),
            pl.BlockSpec((1, FFN_TN), lambda i, j: (0, nj + j)),
            pl.BlockSpec((FFN_TN, D), lambda i, j: (j, 0)),
        ],
        out_specs=row,
        out_shape=jax.ShapeDtypeStruct((N_TOK, D), F32),
        scratch_shapes=[pltpu.VMEM((ROW_BLK, D), BF16), pltpu.VMEM((ROW_BLK, D), F32)],
        compiler_params=pltpu.CompilerParams(
            dimension_semantics=("arbitrary", "arbitrary"), vmem_limit_bytes=VMEM_LIMIT),
        name="conv_ffn",
    )(x, mod, mod, mod, gains, gains, w_up, w_up, conv_w, conv_w, conv_b, conv_b, w_down)


def _out_kernel(y_ref, w_ref, x_ref, gt_ref, g_ref, o_ref):
    y = jnp.dot(y_ref[...], w_ref[...], preferred_element_type=F32)
    o_ref[...] = x_ref[...] + gt_ref[...] * _rms(y, g_ref[...])


def _out_layer(y, w, x, mod, gains, layer):
    row = pl.BlockSpec((ROW_BLK, D), lambda i: (i, 0))
    return pl.pallas_call(
        _out_kernel,
        grid=(N_BLKS,),
        in_specs=[row, pl.BlockSpec((D, D), lambda i: (0, 0)), row, _mod_spec(layer, 2), _gain_spec(layer, 1)],
        out_specs=row,
        out_shape=jax.ShapeDtypeStruct((N_TOK, D), F32),
        compiler_params=pltpu.CompilerParams(
            dimension_semantics=("arbitrary",), vmem_limit_bytes=VMEM_LIMIT),
        name="mixer_out",
    )(y, w, x, mod, gains)


def _dn_in_kernel(x_ref, sh_ref, sc_ref, gpre_ref, w_ref, cw_ref, wab_ref, alog_ref, dtb_ref,
                  p_ref, g_ref, h_scr):
    i, j = pl.program_id(0), pl.program_id(1)

    @pl.when(j == 0)
    def _():
        h = _pre(x_ref[...], gpre_ref[...], sh_ref[...], sc_ref[...]).astype(BF16)
        h_scr[...] = h
        ab = jnp.dot(h, wab_ref[...], preferred_element_type=F32)
        t = ab + dtb_ref[...]
        softplus = jnp.maximum(t, 0.0) + jnp.log(1.0 + jnp.exp(-jnp.abs(t)))
        lane = lax.broadcasted_iota(jnp.int32, ab.shape, 1)
        g_ref[...] = jnp.where(lane < 2 * HEADS_DN, -jnp.exp(alog_ref[...]) * softplus, jax.nn.sigmoid(ab))

    u = jnp.dot(h_scr[...], w_ref[...], preferred_element_type=F32)
    tiles_per_part = D // DN_TN

    @pl.when(j < 3 * tiles_per_part)
    def _():
        first, last = _seq_edges(i, ROW_BLK)
        act = _silu(_conv3(u, cw_ref[...], first, last))

        @pl.when(j < 2 * tiles_per_part)
        def _():
            qscale = jnp.where(j < tiles_per_part, DK ** -0.5, 1.0)
            for s in range(DN_TN // DK):
                xs = act[:, s * DK:(s + 1) * DK]
                inv = lax.rsqrt(jnp.sum(xs * xs, axis=-1, keepdims=True) + EPS) * qscale
                p_ref[:, s * DK:(s + 1) * DK] = (xs * inv).astype(BF16)

        @pl.when(j >= 2 * tiles_per_part)
        def _():
            p_ref[...] = act.astype(BF16)

    @pl.when(j >= 3 * tiles_per_part)
    def _():
        p_ref[...] = u.astype(BF16)


def _dn_in_layer(x, mod, gains, layer, w_main, conv_w, w_ab, a_log, dt_bias):
    row = pl.BlockSpec((ROW_BLK, D), lambda i, j: (i, 0))
    vec = pl.BlockSpec((1, LANES), lambda i, j: (0, 0))
    return pl.pallas_call(
        _dn_in_kernel,
        grid=(N_BLKS, DN_MAIN // DN_TN),
        in_specs=[
            row, _mod_spec(layer, 0), _mod_spec(layer, 1), _gain_spec(layer, 0),
            pl.BlockSpec((D, DN_TN), lambda i, j: (0, j)),
            pl.BlockSpec((3, DN_TN), lambda i, j: (0, j)),
            pl.BlockSpec((D, LANES), lambda i, j: (0, 0)),
            vec, vec,
        ],
        out_specs=[
            pl.BlockSpec((ROW_BLK, DN_TN), lambda i, j: (i, j)),
            pl.BlockSpec((ROW_BLK, LANES), lambda i, j: (i, 0)),
        ],
        out_shape=[jax.ShapeDtypeStruct((N_TOK, DN_MAIN), BF16), jax.ShapeDtypeStruct((N_TOK, LANES), F32)],
        scratch_shapes=[pltpu.VMEM((ROW_BLK, D), BF16)],
        compiler_params=pltpu.CompilerParams(
            dimension_semantics=("arbitrary", "arbitrary"), vmem_limit_bytes=VMEM_LIMIT),
        name="deltanet_in",
    )(x, mod, mod, gains, w_main, conv_w, w_ab, a_log, dt_bias)


def _bdot_nt(a, b):
    return lax.dot_general(a.astype(BF16), b.astype(BF16), (((1,), (1,)), ((), ())), preferred_element_type=F32)


def _bdot_tn(a, b):
    return lax.dot_general(a.astype(BF16), b.astype(BF16), (((0,), (0,)), ((), ())), preferred_element_type=F32)


SOLVE_BLK = 16


def _solve_unit_triangular(tri, rhs, r, c, reverse):
    C, nb = CHUNK, CHUNK // SOLVE_BLK
    blk_r, blk_c = r // SOLVE_BLK, c // SOLVE_BLK
    inv_d = jnp.where(r == c, 1.0, 0.0)
    steps = range(SOLVE_BLK - 1, 0, -1) if reverse else range(SOLVE_BLK - 1)
    for s in steps:
        col = jnp.sum(jnp.where(c == blk_r * SOLVE_BLK + s, tri, 0.0), axis=1, keepdims=True)
        pivot_rows = jnp.concatenate(
            [jnp.broadcast_to(inv_d[b * SOLVE_BLK + s:b * SOLVE_BLK + s + 1, :], (SOLVE_BLK, C)) for b in range(nb)],
            axis=0)
        inv_d = inv_d - col * pivot_rows
    off = jnp.where(blk_r != blk_c, tri, 0.0).astype(BF16)
    inv_d = inv_d.astype(BF16)
    width = rhs.shape[1]

    def placed(block, b):
        parts = []
        if b > 0:
            parts.append(jnp.zeros((b * SOLVE_BLK, width), F32))
        parts.append(block)
        if b < nb - 1:
            parts.append(jnp.zeros(((nb - 1 - b) * SOLVE_BLK, width), F32))
        return jnp.concatenate(parts, axis=0)

    order = range(nb - 1, -1, -1) if reverse else range(nb)
    x = None
    for b in order:
        rows = slice(b * SOLVE_BLK, (b + 1) * SOLVE_BLK)
        y = rhs[rows]
        if x is not None:
            y = y - jnp.dot(off[rows], x.astype(BF16), preferred_element_type=F32)
        xb = placed(jnp.dot(inv_d[rows], placed(y, b).astype(BF16), preferred_element_type=F32), b)
        x = xb if x is None else x + xb
    return x


def _dn_chunk_prepare(q, k, v, g_raw, beta, reverse):
    C = CHUNK
    r = lax.broadcasted_iota(jnp.int32, (C, C), 0)
    c = lax.broadcasted_iota(jnp.int32, (C, C), 1)
    incl = (r <= c) if reverse else (r >= c)
    strict = (r < c) if reverse else (r > c)
    g_mat = jnp.broadcast_to(g_raw, (C, C))
    scanned = (r >= c) if reverse else (r <= c)
    cum_row = jnp.sum(jnp.where(scanned, g_mat, 0.0), axis=0, keepdims=True)
    cum_col = jnp.sum(jnp.where(r == c, jnp.broadcast_to(cum_row, (C, C)), 0.0), axis=1, keepdims=True)
    decay = jnp.exp(jnp.where(incl, cum_col - cum_row, -jnp.inf))
    kb = k * beta
    kk = _bdot_nt(kb, k) * decay
    qk = _bdot_nt(q, k) * decay
    low = jnp.where(strict, kk, 0.0)
    e_col = jnp.exp(cum_col)
    sol = _solve_unit_triangular(low, jnp.concatenate([v * beta, kb * e_col], axis=1), r, c, reverse)
    u, w = sol[:, :DK], sol[:, DK:]
    total = cum_col[0:1] if reverse else cum_col[C - 1:C]
    k_end = k * jnp.exp(total - cum_col)
    return u, w, q * e_col, k_end, qk, jnp.exp(total)


def _dn_scan_kernel(q_ref, k_ref, v_ref, z_ref, g_ref, s0_ref, on_ref, y_ref, *rest, seq_len, with_final):
    if with_final:
        sfin_ref, rest = rest[0], rest[1:]
    u_scr, w_scr, qe_scr, ke_scr, qk_scr, el_scr, o_scr = rest
    n = seq_len // CHUNK
    head = pl.program_id(1)
    lane = lax.broadcasted_iota(jnp.int32, (CHUNK, LANES), 1)

    def prepare(c, carry):
        r0 = pl.multiple_of(c * CHUNK, CHUNK)
        rows = pl.ds(r0, CHUNK)
        q = q_ref[rows, :].astype(F32)
        k = k_ref[rows, :].astype(F32)
        v = v_ref[rows, :].astype(F32)
        gates = g_ref[rows, :]
        for d in range(2):
            g_raw = jnp.sum(jnp.where(lane == d * HEADS_DN + head, gates, 0.0), axis=1, keepdims=True)
            beta = jnp.sum(jnp.where(lane == (2 + d) * HEADS_DN + head, gates, 0.0), axis=1, keepdims=True)
            u, w, qe, ke, qk, el = _dn_chunk_prepare(q, k, v, g_raw, beta, reverse=(d == 1))
            u_scr[d, rows, :] = u
            w_scr[d, rows, :] = w.astype(BF16)
            qe_scr[d, rows, :] = qe.astype(BF16)
            ke_scr[d, rows, :] = ke.astype(BF16)
            qk_scr[d, rows, :] = qk.astype(BF16)
            el_scr[d, c] = jnp.broadcast_to(el, (8, LANES))
        return carry

    lax.fori_loop(0, n, prepare, 0)

    def scan(t, states):
        new_states = []
        for d in range(2):
            c = t if d == 0 else n - 1 - t
            rows = pl.ds(pl.multiple_of(c * CHUNK, CHUNK), CHUNK)
            s = states[d]
            sb = s.astype(BF16)
            v_new = u_scr[d, rows, :] - jnp.dot(w_scr[d, rows, :], sb, preferred_element_type=F32)
            vb = v_new.astype(BF16)
            o_scr[d, rows, :] = (jnp.dot(qe_scr[d, rows, :], sb, preferred_element_type=F32)
                                 + jnp.dot(qk_scr[d, rows, :], vb, preferred_element_type=F32))
            new_states.append(s * el_scr[d, c][0:1, :] + _bdot_tn(ke_scr[d, rows, :], vb))
        return tuple(new_states)

    s_fwd, s_bwd = lax.fori_loop(0, n, scan, (s0_ref[0], s0_ref[1]))
    if with_final:
        sfin_ref[0] = s_fwd
        sfin_ref[1] = s_bwd
    o = o_scr[0] + o_scr[1]
    y_ref[...] = (_rms(o, on_ref[...]) * _silu(z_ref[...].astype(F32))).astype(BF16)


def _dn_scan(p, gates, s0, s0_index, onorm_g, *, n_seq, seq_len, row_offset, with_final):
    blk0 = row_offset // seq_len

    def part(k):
        return pl.BlockSpec((seq_len, DK), lambda b, h: (blk0 + b, k * HEADS_DN + h))

    out_specs = [pl.BlockSpec((seq_len, DK), lambda b, h: (b, h))]
    out_shape = [jax.ShapeDtypeStruct((n_seq * seq_len, D), BF16)]
    if with_final:
        out_specs.append(pl.BlockSpec((None, 2, None, DK, DK), lambda b, h: (b, 0, h, 0, 0)))
        out_shape.append(jax.ShapeDtypeStruct((n_seq, 2, HEADS_DN, DK, DK), F32))
    n = seq_len // CHUNK
    res = pl.pallas_call(
        functools.partial(_dn_scan_kernel, seq_len=seq_len, with_final=with_final),
        grid=(n_seq, HEADS_DN),
        in_specs=[
            part(0), part(1), part(2), part(3),
            pl.BlockSpec((seq_len, LANES), lambda b, h: (blk0 + b, 0)),
            pl.BlockSpec((None,) * (s0.ndim - 4) + (2, None, DK, DK), s0_index),
            pl.BlockSpec((1, DK), lambda b, h: (0, 0)),
        ],
        out_specs=out_specs,
        out_shape=out_shape,
        scratch_shapes=[
            pltpu.VMEM((2, seq_len, DK), F32),
            pltpu.VMEM((2, seq_len, DK), BF16),
            pltpu.VMEM((2, seq_len, DK), BF16),
            pltpu.VMEM((2, seq_len, DK), BF16),
            pltpu.VMEM((2, seq_len, CHUNK), BF16),
            pltpu.VMEM((2, n, 8, LANES), F32),
            pltpu.VMEM((2, seq_len, DK), F32),
        ],
        compiler_params=pltpu.CompilerParams(
            dimension_semantics=("arbitrary", "arbitrary"), vmem_limit_bytes=VMEM_LIMIT),
        name="deltanet_scan_%d" % seq_len,
    )(p, p, p, p, gates, s0, onorm_g)
    return res


def _rope_tile(x, cos, sin):
    lane = lax.broadcasted_iota(jnp.int32, x.shape, 1)
    width = x.shape[1]
    partner = jnp.where((lane & 31) < 16, pltpu.roll(x, width - 16, 1), pltpu.roll(x, 16, 1))
    return x * cos + partner * sin


def _at_in_kernel(x_ref, sh_ref, sc_ref, gpre_ref, w_ref, cos_ref, sin_ref, q_ref, kv_ref, h_scr):
    i, j = pl.program_id(0), pl.program_id(1)
    q_tiles = AT_Q // AT_TN

    @pl.when(j == 0)
    def _():
        h_scr[...] = _pre(x_ref[...], gpre_ref[...], sh_ref[...], sc_ref[...]).astype(BF16)

    u = jnp.dot(h_scr[...], w_ref[...], preferred_element_type=F32)

    def roped():
        return jnp.concatenate(
            [_rope_tile(u[:, s * LANES:(s + 1) * LANES], cos_ref[...], sin_ref[...]) for s in range(AT_TN // LANES)],
            axis=1)

    is_latent = i >= PROMPT_BLKS

    @pl.when(jnp.logical_and(j < q_tiles, is_latent))
    def _():
        q_ref[...] = roped().astype(BF16)

    @pl.when(jnp.logical_and(j < q_tiles, jnp.logical_not(is_latent)))
    def _():
        q_ref[...] = u.astype(BF16)

    @pl.when(jnp.logical_and(j == q_tiles, is_latent))
    def _():
        kv_ref[...] = roped()

    @pl.when(jnp.logical_or(j > q_tiles, jnp.logical_and(j == q_tiles, jnp.logical_not(is_latent))))
    def _():
        kv_ref[...] = u


def _at_in_layer(x, mod, gains, layer, w_qkv, cos, sin):
    q_tiles = AT_Q // AT_TN
    row = pl.BlockSpec((ROW_BLK, D), lambda i, j: (i, 0))
    tab = pl.BlockSpec((ROW_BLK, LANES), lambda i, j: (0, 0))
    return pl.pallas_call(
        _at_in_kernel,
        grid=(N_BLKS, (AT_Q + 2 * AT_KVW) // AT_TN),
        in_specs=[row, _mod_spec(layer, 0), _mod_spec(layer, 1), _gain_spec(layer, 0),
                  pl.BlockSpec((D, AT_TN), lambda i, j: (0, j)), tab, tab],
        out_specs=[
            pl.BlockSpec((ROW_BLK, AT_TN), lambda i, j: (i, jnp.minimum(j, q_tiles - 1))),
            pl.BlockSpec((ROW_BLK, AT_TN), lambda i, j: (i, jnp.maximum(j - q_tiles, 0))),
        ],
        out_shape=[jax.ShapeDtypeStruct((N_TOK, AT_Q), BF16), jax.ShapeDtypeStruct((N_TOK, 2 * AT_KVW), F32)],
        scratch_shapes=[pltpu.VMEM((ROW_BLK, D), BF16)],
        compiler_params=pltpu.CompilerParams(
            dimension_semantics=("arbitrary", "arbitrary"), vmem_limit_bytes=VMEM_LIMIT),
        name="attention_in",
    )(x, mod, mod, gains, w_qkv, cos, sin)


def _softmax_pv(scores, values, sink):
    m = sink
    for s in scores:
        m = jnp.maximum(m, jnp.max(s, axis=-1, keepdims=True))
    denom = jnp.exp(sink - m)
    acc = None
    for s, v in zip(scores, values):
        p = jnp.exp(s - m)
        denom = denom + jnp.sum(p, axis=-1, keepdims=True)
        pv = jnp.dot(p.astype(BF16), v, preferred_element_type=F32)
        acc = pv if acc is None else acc + pv
    return acc / denom


def _ctx_attn_kernel(q_ref, kv_ref, sink_ref, y_ref):
    scale = HD ** -0.5
    outs = []
    for kvh in range(AT_KV):
        k = kv_ref[:, kvh * HD:(kvh + 1) * HD].astype(BF16)
        v = kv_ref[:, AT_KVW + kvh * HD:AT_KVW + (kvh + 1) * HD].astype(BF16)
        for g in range(AT_G):
            hh = kvh * AT_G + g
            s = _bdot_nt(q_ref[:, hh * HD:(hh + 1) * HD], k) * scale
            outs.append(_softmax_pv([s], [v], sink_ref[0:1, hh:hh + 1]))
    y_ref[...] = jnp.concatenate(outs, axis=1).astype(BF16)


def _ctx_attention(q, kv, sink):
    nq = L_PROMPT // QBLK
    return pl.pallas_call(
        _ctx_attn_kernel,
        grid=(N_PROMPT, nq),
        in_specs=[
            pl.BlockSpec((QBLK, AT_Q), lambda b, t: (b * nq + t, 0)),
            pl.BlockSpec((L_PROMPT, 2 * AT_KVW), lambda b, t: (b, 0)),
            pl.BlockSpec((1, LANES), lambda b, t: (0, 0)),
        ],
        out_specs=pl.BlockSpec((QBLK, AT_Q), lambda b, t: (b * nq + t, 0)),
        out_shape=jax.ShapeDtypeStruct((TOK_PROMPT, AT_Q), BF16),
        compiler_params=pltpu.CompilerParams(dimension_semantics=("arbitrary", "arbitrary")),
        name="context_attention",
    )(q, kv, sink)


def _lat_attn_kernel(q_ref, kv_ref, ck_ref, cv_ref, sink_ref, y_ref):
    scale = HD ** -0.5
    t = pl.program_id(1)
    k0 = pl.multiple_of(jnp.clip(t * QBLK - WINDOW, 0, L_LATENT - BAND), QBLK)
    qpos = t * QBLK + lax.broadcasted_iota(jnp.int32, (QBLK, BAND), 0)
    kpos = k0 + lax.broadcasted_iota(jnp.int32, (QBLK, BAND), 1)
    valid = jnp.abs(qpos - kpos) <= WINDOW
    outs = []
    for kvh in range(AT_KV):
        k_loc = kv_ref[pl.ds(k0, BAND), kvh * HD:(kvh + 1) * HD].astype(BF16)
        v_loc = kv_ref[pl.ds(k0, BAND), AT_KVW + kvh * HD:AT_KVW + (kvh + 1) * HD].astype(BF16)
        k_ctx = ck_ref[:, kvh * HD:(kvh + 1) * HD].astype(BF16)
        v_ctx = cv_ref[:, kvh * HD:(kvh + 1) * HD].astype(BF16)
        for g in range(AT_G):
            hh = kvh * AT_G + g
            qh = q_ref[:, hh * HD:(hh + 1) * HD]
            s_loc = jnp.where(valid, _bdot_nt(qh, k_loc) * scale, -jnp.inf)
            s_ctx = _bdot_nt(qh, k_ctx) * scale
            outs.append(_softmax_pv([s_loc, s_ctx], [v_loc, v_ctx], sink_ref[0:1, hh:hh + 1]))
    y_ref[...] = jnp.concatenate(outs, axis=1).astype(BF16)


def _lat_attention(q, kv, cache_k, cache_v, sink, attn_layer):
    nq = L_LATENT // QBLK
    q0 = TOK_PROMPT // QBLK
    s0 = TOK_PROMPT // L_LATENT
    cache = pl.BlockSpec((None, None, PAST, AT_KVW), lambda b, t: (b, attn_layer, 0, 0))
    return pl.pallas_call(
        _lat_attn_kernel,
        grid=(N_LATENT, nq),
        in_specs=[
            pl.BlockSpec((QBLK, AT_Q), lambda b, t: (q0 + b * nq + t, 0)),
            pl.BlockSpec((L_LATENT, 2 * AT_KVW), lambda b, t: (s0 + b, 0)),
            cache, cache,
            pl.BlockSpec((1, LANES), lambda b, t: (0, 0)),
        ],
        out_specs=pl.BlockSpec((QBLK, AT_Q), lambda b, t: (b * nq + t, 0)),
        out_shape=jax.ShapeDtypeStruct((N_LATENT * L_LATENT, AT_Q), BF16),
        compiler_params=pltpu.CompilerParams(dimension_semantics=("arbitrary", "arbitrary")),
        name="latent_attention",
    )(q, kv, cache_k, cache_v, sink)


def _rope_tables():
    pos = jnp.arange(L_LATENT)
    half = HD // 2
    inv = jnp.power(ROPE_BASE, -jnp.arange(0, half, 2, dtype=F32) / half)
    ang_row = (pos // GRID_W).astype(F32)[:, None] * inv
    ang_col = (pos % GRID_W).astype(F32)[:, None] * inv
    ang = jnp.concatenate([ang_row, ang_row, ang_col, ang_col], axis=1)
    sign = jnp.tile(jnp.concatenate([-jnp.ones((16,), F32), jnp.ones((16,), F32)]), 2)
    cos = jnp.tile(jnp.cos(ang), (1, 2))
    sin = jnp.tile(jnp.sin(ang) * sign, (1, 2))
    return cos, sin


def _pad_lanes(v):
    v = v.reshape(1, -1).astype(F32)
    return jnp.pad(v, ((0, 0), (0, LANES - v.shape[1])))


def kernel(x_prompt, x_sample, state_delta, cache_k, cache_v, c, c_ctx, w_ada, b_ada, norm_g, dn_w_in, dn_conv_w, dn_a_log, dn_dt_bias, dn_onorm_g, dn_w_out, at_w_qkv, at_sink, at_w_o, ffn_w_up, ffn_conv_w, ffn_conv_b, ffn_w_down):
    x = jnp.concatenate([x_prompt.reshape(TOK_PROMPT, D), x_sample.reshape(N_LATENT * L_LATENT, D)], axis=0)
    cvecs = jnp.concatenate([c_ctx[None, :], c, jnp.zeros((MOD_ROWS - 1 - N_LATENT, D), F32)], axis=0)
    mod = _modulation(cvecs, w_ada, b_ada).reshape(DEPTH, MOD_ROWS, 1, 6 * D)
    gains = norm_g.reshape(DEPTH, 4, 1, D)
    cos, sin = _rope_tables()
    ck = cache_k.reshape(N_LATENT, -1, PAST, AT_KVW)
    cv = cache_v.reshape(N_LATENT, -1, PAST, AT_KVW)
    zero_state = jnp.zeros((2, HEADS_DN, DK, DK), F32)

    states, new_k, new_v = [], [], []
    for layer in range(DEPTH):
        j = layer // 2
        if layer % 2 == 0:
            w_in = dn_w_in[j]
            w_main = w_in[:, :DN_MAIN].astype(BF16)
            w_ab = jnp.pad(w_in[:, DN_MAIN:], ((0, 0), (0, LANES - 4 * HEADS_DN))).astype(BF16)
            conv_w = jnp.pad(dn_conv_w[j], ((0, 0), (0, D)))
            p, gates = _dn_in_layer(x, mod, gains, layer, w_main, conv_w, w_ab,
                                    _pad_lanes(dn_a_log[j]), _pad_lanes(dn_dt_bias[j]))
            onorm = dn_onorm_g[j].reshape(1, DK)
            y_p, s_fin = _dn_scan(p, gates, zero_state, lambda b, h: (0, h, 0, 0), onorm,
                                  n_seq=N_PROMPT, seq_len=L_PROMPT, row_offset=0, with_final=True)
            (y_s,) = _dn_scan(p, gates, state_delta, lambda b, h, j=j: (b, j, 0, h, 0, 0), onorm,
                              n_seq=N_LATENT, seq_len=L_LATENT, row_offset=TOK_PROMPT, with_final=False)
            states.append(s_fin)
            w_o = dn_w_out[j]
        else:
            q, kv = _at_in_layer(x, mod, gains, layer, at_w_qkv[j].astype(BF16), cos, sin)
            sink = _pad_lanes(at_sink[j])
            y_p = _ctx_attention(q, kv, sink)
            y_s = _lat_attention(q, kv, ck, cv, sink, j)
            new_k.append(kv[:TOK_PROMPT, :AT_KVW].reshape(N_PROMPT, L_PROMPT, AT_KV, HD))
            new_v.append(kv[:TOK_PROMPT, AT_KVW:].reshape(N_PROMPT, L_PROMPT, AT_KV, HD))
            w_o = at_w_o[j]
        y = jnp.concatenate([y_p, y_s], axis=0)
        x = _out_layer(y, w_o.astype(BF16), x, mod, gains, layer)
        x = _ffn_layer(x, mod, gains, layer, ffn_w_up[layer].astype(BF16), ffn_conv_w[layer],
                       ffn_conv_b[layer].reshape(1, 2 * FFN), ffn_w_down[layer].astype(BF16))

    y_prompt = x[:TOK_PROMPT].reshape(N_PROMPT, L_PROMPT, D)
    y_sample = x[TOK_PROMPT:].reshape(N_LATENT, L_LATENT, D)
    return (y_prompt, y_sample, jnp.stack(states, axis=1), jnp.stack(new_k, axis=1), jnp.stack(new_v, axis=1))
```

```python
import functools

import jax
import jax.numpy as jnp
from jax import lax
from jax.experimental import pallas as pl
from jax.experimental.pallas import tpu as pltpu

F32 = jnp.float32
BF16 = jnp.bfloat16

D = 1024
N_PROMPT, L_PROMPT = 16, 256
N_LATENT, L_LATENT = 8, 1024
PAST = 512
GRID_W = 64
DEPTH = 4
HEADS_DN, DK = 8, 128
CHUNK = 64
DN_MAIN = 4 * D
AT_HEADS, AT_KV, AT_G, HD = 16, 4, 4, 64
AT_Q = AT_HEADS * HD
AT_KVW = AT_KV * HD
WINDOW = 128
QBLK = 128
BAND = QBLK + 2 * WINDOW
FFN = 2816
EPS = 1e-6
ROPE_BASE = 10000.0

ROW_BLK = 1024
TOK_PROMPT = N_PROMPT * L_PROMPT
N_TOK = TOK_PROMPT + N_LATENT * L_LATENT
PROMPT_BLKS = TOK_PROMPT // ROW_BLK
N_BLKS = N_TOK // ROW_BLK
MOD_ROWS = 16
LANES = 128
FFN_TN = 256
DN_TN = 512
AT_TN = 256
VMEM_LIMIT = 56 * 1024 * 1024


def _mod_row(i):
    return jnp.where(i < PROMPT_BLKS, 0, i - (PROMPT_BLKS - 1))


def _mod_spec(layer, chunk):
    return pl.BlockSpec((None, None, 1, D), lambda i, *_: (layer, _mod_row(i), 0, chunk))


def _gain_spec(layer, k):
    return pl.BlockSpec((None, None, 1, D), lambda i, *_: (layer, k, 0, 0))


def _rms(x, g):
    return x * lax.rsqrt(jnp.mean(x * x, axis=-1, keepdims=True) + EPS) * g


def _pre(x, g, shift, scale):
    return _rms(x, g) * (1.0 + scale) + shift


def _silu(x):
    return x * jax.nn.sigmoid(x)


def _seq_edges(i, rows):
    seq_len = jnp.where(i < PROMPT_BLKS, L_PROMPT, L_LATENT)
    pos = lax.broadcasted_iota(jnp.int32, (rows, 1), 0) & (seq_len - 1)
    return pos == 0, pos == seq_len - 1


def _conv3(u, w, first, last):
    rows = u.shape[0]
    prev = jnp.where(first, 0.0, pltpu.roll(u, 1, 0))
    nxt = jnp.where(last, 0.0, pltpu.roll(u, rows - 1, 0))
    return prev * w[0:1] + u * w[1:2] + nxt * w[2:3]


def _mod_kernel(c_ref, w_ref, b_ref, o_ref):
    s = _silu(c_ref[...]).astype(BF16)
    o_ref[...] = jnp.dot(s, w_ref[...].astype(BF16), preferred_element_type=F32) + b_ref[...]


def _modulation(cvecs, w_ada, b_ada):
    tn = 1536
    return pl.pallas_call(
        _mod_kernel,
        grid=(DEPTH, 6 * D // tn),
        in_specs=[
            pl.BlockSpec((MOD_ROWS, D), lambda l, j: (0, 0)),
            pl.BlockSpec((None, D, tn), lambda l, j: (l, 0, j)),
            pl.BlockSpec((None, 1, tn), lambda l, j: (l, 0, j)),
        ],
        out_specs=pl.BlockSpec((None, MOD_ROWS, tn), lambda l, j: (l, 0, j)),
        out_shape=jax.ShapeDtypeStruct((DEPTH, MOD_ROWS, 6 * D), F32),
        compiler_params=pltpu.CompilerParams(
            dimension_semantics=("arbitrary", "arbitrary"), vmem_limit_bytes=VMEM_LIMIT),
        name="modulation",
    )(cvecs, w_ada, b_ada.reshape(DEPTH, 1, 6 * D))


def _ffn_kernel(x_ref, sh_ref, sc_ref, gt_ref, gpre_ref, gpost_ref, wa_ref, wb_ref, cwa_ref, cwb_ref,
                cba_ref, cbb_ref, wd_ref, o_ref, h_scr, acc_scr):
    i, j = pl.program_id(0), pl.program_id(1)

    @pl.when(j == 0)
    def _():
        h_scr[...] = _pre(x_ref[...], gpre_ref[...], sh_ref[...], sc_ref[...]).astype(BF16)
        acc_scr[...] = jnp.zeros_like(acc_scr)

    h = h_scr[...]
    first, last = _seq_edges(i, ROW_BLK)
    a = _conv3(jnp.dot(h, wa_ref[...], preferred_element_type=F32), cwa_ref[...], first, last) + cba_ref[...]
    b = _conv3(jnp.dot(h, wb_ref[...], preferred_element_type=F32), cwb_ref[...], first, last) + cbb_ref[...]
    act = (_silu(a) * b).astype(BF16)
    acc_scr[...] += jnp.dot(act, wd_ref[...], preferred_element_type=F32)

    @pl.when(j == pl.num_programs(1) - 1)
    def _():
        o_ref[...] = x_ref[...] + gt_ref[...] * _rms(acc_scr[...], gpost_ref[...])


def _ffn_layer(x, mod, gains, layer, w_up, conv_w, conv_b, w_down):
    nj = FFN // FFN_TN
    row = pl.BlockSpec((ROW_BLK, D), lambda i, j: (i, 0))
    return pl.pallas_call(
        _ffn_kernel,
        grid=(N_BLKS, nj),
        in_specs=[
            row, _mod_spec(layer, 3), _mod_spec(layer, 4), _mod_spec(layer, 5),
            _gain_spec(layer, 2), _gain_spec(layer, 3),
            pl.BlockSpec((D, FFN_TN), lambda i, j: (0, j)),
            pl.BlockSpec((D, FFN_TN), lambda i, j: (0, nj + j)),
            pl.BlockSpec((3, FFN_TN), lambda i, j: (0, j)),
            pl.BlockSpec((3, FFN_TN), lambda i, j: (0, nj + j)),
            pl.BlockSpec((1, FFN_TN), lambda i, j: (0, j)),
            pl.BlockSpec((1, FFN_TN), lambda i, j: (0, nj + j)),
            pl.BlockSpec((FFN_TN, D), lambda i, j: (j, 0)),
        ],
        out_specs=row,
        out_shape=jax.ShapeDtypeStruct((N_TOK, D), F32),
        scratch_shapes=[pltpu.VMEM((ROW_BLK, D), BF16), pltpu.VMEM((ROW_BLK, D), F32)],
        compiler_params=pltpu.CompilerParams(
            dimension_semantics=("arbitrary", "arbitrary"), vmem_limit_bytes=VMEM_LIMIT),
        name="conv_ffn",
    )(x, mod, mod, mod, gains, gains, w_up, w_up, conv_w, conv_w, conv_b, conv_b, w_down)


def _out_kernel(y_ref, w_ref, x_ref, gt_ref, g_ref, o_ref):
    y = jnp.dot(y_ref[...], w_ref[...], preferred_element_type=F32)
    o_ref[...] = x_ref[...] + gt_ref[...] * _rms(y, g_ref[...])


def _out_layer(y, w, x, mod, gains, layer):
    row = pl.BlockSpec((ROW_BLK, D), lambda i: (i, 0))
    return pl.pallas_call(
        _out_kernel,
        grid=(N_BLKS,),
        in_specs=[row, pl.BlockSpec((D, D), lambda i: (0, 0)), row, _mod_spec(layer, 2), _gain_spec(layer, 1)],
        out_specs=row,
        out_shape=jax.ShapeDtypeStruct((N_TOK, D), F32),
        compiler_params=pltpu.CompilerParams(
            dimension_semantics=("arbitrary",), vmem_limit_bytes=VMEM_LIMIT),
        name="mixer_out",
    )(y, w, x, mod, gains)


def _dn_in_kernel(x_ref, sh_ref, sc_ref, gpre_ref, w_ref, cw_ref, wab_ref, alog_ref, dtb_ref,
                  p_ref, g_ref, h_scr):
    i, j = pl.program_id(0), pl.program_id(1)

    @pl.when(j == 0)
    def _():
        h = _pre(x_ref[...], gpre_ref[...], sh_ref[...], sc_ref[...]).astype(BF16)
        h_scr[...] = h
        ab = jnp.dot(h, wab_ref[...], preferred_element_type=F32)
        t = ab + dtb_ref[...]
        softplus = jnp.maximum(t, 0.0) + jnp.log(1.0 + jnp.exp(-jnp.abs(t)))
        lane = lax.broadcasted_iota(jnp.int32, ab.shape, 1)
        g_ref[...] = jnp.where(lane < 2 * HEADS_DN, -jnp.exp(alog_ref[...]) * softplus, jax.nn.sigmoid(ab))

    u = jnp.dot(h_scr[...], w_ref[...], preferred_element_type=F32)
    tiles_per_part = D // DN_TN

    @pl.when(j < 3 * tiles_per_part)
    def _():
        first, last = _seq_edges(i, ROW_BLK)
        act = _silu(_conv3(u, cw_ref[...], first, last))

        @pl.when(j < 2 * tiles_per_part)
        def _():
            qscale = jnp.where(j < tiles_per_part, DK ** -0.5, 1.0)
            for s in range(DN_TN // DK):
                xs = act[:, s * DK:(s + 1) * DK]
                inv = lax.rsqrt(jnp.sum(xs * xs, axis=-1, keepdims=True) + EPS) * qscale
                p_ref[:, s * DK:(s + 1) * DK] = (xs * inv).astype(BF16)

        @pl.when(j >= 2 * tiles_per_part)
        def _():
            p_ref[...] = act.astype(BF16)

    @pl.when(j >= 3 * tiles_per_part)
    def _():
        p_ref[...] = u.astype(BF16)


def _dn_in_layer(x, mod, gains, layer, w_main, conv_w, w_ab, a_log, dt_bias):
    row = pl.BlockSpec((ROW_BLK, D), lambda i, j: (i, 0))
    vec = pl.BlockSpec((1, LANES), lambda i, j: (0, 0))
    return pl.pallas_call(
        _dn_in_kernel,
        grid=(N_BLKS, DN_MAIN // DN_TN),
        in_specs=[
            row, _mod_spec(layer, 0), _mod_spec(layer, 1), _gain_spec(layer, 0),
            pl.BlockSpec((D, DN_TN), lambda i, j: (0, j)),
            pl.BlockSpec((3, DN_TN), lambda i, j: (0, j)),
            pl.BlockSpec((D, LANES), lambda i, j: (0, 0)),
            vec, vec,
        ],
        out_specs=[
            pl.BlockSpec((ROW_BLK, DN_TN), lambda i, j: (i, j)),
            pl.BlockSpec((ROW_BLK, LANES), lambda i, j: (i, 0)),
        ],
        out_shape=[jax.ShapeDtypeStruct((N_TOK, DN_MAIN), BF16), jax.ShapeDtypeStruct((N_TOK, LANES), F32)],
        scratch_shapes=[pltpu.VMEM((ROW_BLK, D), BF16)],
        compiler_params=pltpu.CompilerParams(
            dimension_semantics=("arbitrary", "arbitrary"), vmem_limit_bytes=VMEM_LIMIT),
        name="deltanet_in",
    )(x, mod, mod, gains, w_main, conv_w, w_ab, a_log, dt_bias)


def _bdot_nt(a, b):
    return lax.dot_general(a.astype(BF16), b.astype(BF16), (((1,), (1,)), ((), ())), preferred_element_type=F32)


def _bdot_tn(a, b):
    return lax.dot_general(a.astype(BF16), b.astype(BF16), (((0,), (0,)), ((), ())), preferred_element_type=F32)


SOLVE_BLK = 16


def _bmm(a, b):
    return jnp.einsum('gmk,gkn->gmn', a.astype(BF16), b.astype(BF16), preferred_element_type=F32)


def _bmm_nt(a, b):
    return jnp.einsum('gmk,gnk->gmn', a.astype(BF16), b.astype(BF16), preferred_element_type=F32)


def _solve_unit_triangular(tri, rhs, r, c, reverse):
    G, C, nb = tri.shape[0], CHUNK, CHUNK // SOLVE_BLK
    blk_r, blk_c = r // SOLVE_BLK, c // SOLVE_BLK
    inv_d = jnp.broadcast_to(jnp.where(r == c, 1.0, 0.0), (G, C, C))
    steps = range(SOLVE_BLK - 1, 0, -1) if reverse else range(SOLVE_BLK - 1)
    for s in steps:
        col = jnp.sum(jnp.where(c == blk_r * SOLVE_BLK + s, tri, 0.0), axis=2, keepdims=True)
        pivot_rows = jnp.concatenate(
            [jnp.broadcast_to(inv_d[:, b * SOLVE_BLK + s:b * SOLVE_BLK + s + 1, :], (G, SOLVE_BLK, C))
             for b in range(nb)], axis=1)
        inv_d = inv_d - col * pivot_rows
    off = jnp.where(blk_r != blk_c, tri, 0.0).astype(BF16)
    inv_d = inv_d.astype(BF16)
    width = rhs.shape[2]

    def placed(block, b):
        parts = []
        if b > 0:
            parts.append(jnp.zeros((G, b * SOLVE_BLK, width), F32))
        parts.append(block)
        if b < nb - 1:
            parts.append(jnp.zeros((G, (nb - 1 - b) * SOLVE_BLK, width), F32))
        return jnp.concatenate(parts, axis=1)

    order = range(nb - 1, -1, -1) if reverse else range(nb)
    x = None
    for b in order:
        rows = slice(b * SOLVE_BLK, (b + 1) * SOLVE_BLK)
        y = rhs[:, rows]
        if x is not None:
            y = y - _bmm(off[:, rows], x)
        xb = placed(_bmm(inv_d[:, rows], placed(y, b)), b)
        x = xb if x is None else x + xb
    return x


def _dn_chunk_prepare(q, k, v, g_raw, beta, reverse):
    G, C = q.shape[0], CHUNK
    r = lax.broadcasted_iota(jnp.int32, (C, C), 0)
    c = lax.broadcasted_iota(jnp.int32, (C, C), 1)
    incl = (r <= c) if reverse else (r >= c)
    strict = (r < c) if reverse else (r > c)
    g_mat = jnp.broadcast_to(g_raw, (G, C, C))
    scanned = (r >= c) if reverse else (r <= c)
    cum_row = jnp.sum(jnp.where(scanned, g_mat, 0.0), axis=1, keepdims=True)
    cum_col = jnp.sum(jnp.where(r == c, jnp.broadcast_to(cum_row, (G, C, C)), 0.0), axis=2, keepdims=True)
    decay = jnp.exp(jnp.where(incl, cum_col - cum_row, -jnp.inf))
    kb = k * beta
    kk = _bmm_nt(kb, k) * decay
    qk = _bmm_nt(q, k) * decay
    low = jnp.where(strict, kk, 0.0)
    e_col = jnp.exp(cum_col)
    sol = _solve_unit_triangular(low, jnp.concatenate([v * beta, kb * e_col], axis=2), r, c, reverse)
    u, w = sol[:, :, :DK], sol[:, :, DK:]
    total = cum_col[:, 0:1] if reverse else cum_col[:, C - 1:C]
    k_end = k * jnp.exp(total - cum_col)
    return u, w, q * e_col, k_end, qk, jnp.exp(total)


PREP_CHUNKS = 4
HEAD_GRP = 4


def _dn_scan_kernel(q_ref, k_ref, v_ref, z_ref, g_ref, s0_ref, on_ref, y_ref, *rest, seq_len, with_final):
    if with_final:
        sfin_ref, rest = rest[0], rest[1:]
    u_scr, w_scr, qe_scr, ke_scr, qk_scr, el_scr, o_scr, s_scr = rest
    n = seq_len // CHUNK
    G, HG = PREP_CHUNKS, HEAD_GRP
    groups = n // G
    head0 = pl.program_id(1) * HG
    lane = lax.broadcasted_iota(jnp.int32, (G, CHUNK, LANES), 2)

    def prepare(it, carry):
        hh, grp = it // groups, it % groups
        rows = pl.ds(pl.multiple_of(grp * (G * CHUNK), G * CHUNK), G * CHUNK)
        cols = pl.ds(pl.multiple_of(hh * DK, DK), DK)
        q = q_ref[rows, cols].astype(F32).reshape(G, CHUNK, DK)
        k = k_ref[rows, cols].astype(F32).reshape(G, CHUNK, DK)
        v = v_ref[rows, cols].astype(F32).reshape(G, CHUNK, DK)
        gates = g_ref[rows, :].reshape(G, CHUNK, LANES)
        head = head0 + hh
        for d in range(2):
            g_raw = jnp.sum(jnp.where(lane == d * HEADS_DN + head, gates, 0.0), axis=2, keepdims=True)
            beta = jnp.sum(jnp.where(lane == (2 + d) * HEADS_DN + head, gates, 0.0), axis=2, keepdims=True)
            u, w, qe, ke, qk, el = _dn_chunk_prepare(q, k, v, g_raw, beta, reverse=(d == 1))
            u_scr[d, hh, rows, :] = u.reshape(G * CHUNK, DK)
            w_scr[d, hh, rows, :] = w.reshape(G * CHUNK, DK).astype(BF16)
            qe_scr[d, hh, rows, :] = qe.reshape(G * CHUNK, DK).astype(BF16)
            ke_scr[d, hh, rows, :] = ke.reshape(G * CHUNK, DK).astype(BF16)
            qk_scr[d, hh, rows, :] = qk.reshape(G * CHUNK, CHUNK).astype(BF16)
            el_scr[d, hh, pl.ds(grp * G, G)] = jnp.broadcast_to(el, (G, 8, LANES))
        return carry

    lax.fori_loop(0, HG * groups, prepare, 0)
    s_scr[...] = s0_ref[...]

    def scan(t, carry):
        for hh in range(HG):
            for d in range(2):
                c = t if d == 0 else n - 1 - t
                rows = pl.ds(pl.multiple_of(c * CHUNK, CHUNK), CHUNK)
                s = s_scr[d, hh]
                sb = s.astype(BF16)
                v_new = u_scr[d, hh, rows, :] - jnp.dot(w_scr[d, hh, rows, :], sb, preferred_element_type=F32)
                vb = v_new.astype(BF16)
                o_scr[d, hh, rows, :] = (jnp.dot(qe_scr[d, hh, rows, :], sb, preferred_element_type=F32)
                                         + jnp.dot(qk_scr[d, hh, rows, :], vb, preferred_element_type=F32))
                s_scr[d, hh] = s * el_scr[d, hh, c][0:1, :] + _bdot_tn(ke_scr[d, hh, rows, :], vb)
        return carry

    lax.fori_loop(0, n, scan, 0)
    if with_final:
        sfin_ref[...] = s_scr[...]
    for hh in range(HG):
        cols = slice(hh * DK, (hh + 1) * DK)
        o = o_scr[0, hh] + o_scr[1, hh]
        y_ref[:, cols] = (_rms(o, on_ref[...]) * _silu(z_ref[:, cols].astype(F32))).astype(BF16)


def _dn_scan(p, gates, s0, s0_index, onorm_g, *, n_seq, seq_len, row_offset, with_final):
    blk0 = row_offset // seq_len
    HG = HEAD_GRP
    n_grp = HEADS_DN // HG

    def part(k):
        return pl.BlockSpec((seq_len, HG * DK), lambda b, h: (blk0 + b, k * n_grp + h))

    out_specs = [pl.BlockSpec((seq_len, HG * DK), lambda b, h: (b, h))]
    out_shape = [jax.ShapeDtypeStruct((n_seq * seq_len, D), BF16)]
    if with_final:
        out_specs.append(pl.BlockSpec((None, 2, HG, DK, DK), lambda b, h: (b, 0, h, 0, 0)))
        out_shape.append(jax.ShapeDtypeStruct((n_seq, 2, HEADS_DN, DK, DK), F32))
    n = seq_len // CHUNK
    res = pl.pallas_call(
        functools.partial(_dn_scan_kernel, seq_len=seq_len, with_final=with_final),
        grid=(n_seq, n_grp),
        in_specs=[
            part(0), part(1), part(2), part(3),
            pl.BlockSpec((seq_len, LANES), lambda b, h: (blk0 + b, 0)),
            pl.BlockSpec((None,) * (s0.ndim - 4) + (2, HG, DK, DK), s0_index),
            pl.BlockSpec((1, DK), lambda b, h: (0, 0)),
        ],
        out_specs=out_specs,
        out_shape=out_shape,
        scratch_shapes=[
            pltpu.VMEM((2, HG, seq_len, DK), F32),
            pltpu.VMEM((2, HG, seq_len, DK), BF16),
            pltpu.VMEM((2, HG, seq_len, DK), BF16),
            pltpu.VMEM((2, HG, seq_len, DK), BF16),
            pltpu.VMEM((2, HG, seq_len, CHUNK), BF16),
            pltpu.VMEM((2, HG, n, 8, LANES), F32),
            pltpu.VMEM((2, HG, seq_len, DK), F32),
            pltpu.VMEM((2, HG, DK, DK), F32),
        ],
        compiler_params=pltpu.CompilerParams(
            dimension_semantics=("arbitrary", "arbitrary"), vmem_limit_bytes=VMEM_LIMIT),
        name="deltanet_scan_%d" % seq_len,
    )(p, p, p, p, gates, s0, onorm_g)
    return res


def _rope_tile(x, cos, sin):
    lane = lax.broadcasted_iota(jnp.int32, x.shape, 1)
    width = x.shape[1]
    partner = jnp.where((lane & 31) < 16, pltpu.roll(x, width - 16, 1), pltpu.roll(x, 16, 1))
    return x * cos + partner * sin


def _at_in_kernel(x_ref, sh_ref, sc_ref, gpre_ref, w_ref, cos_ref, sin_ref, q_ref, kv_ref, h_scr):
    i, j = pl.program_id(0), pl.program_id(1)
    q_tiles = AT_Q // AT_TN

    @pl.when(j == 0)
    def _():
        h_scr[...] = _pre(x_ref[...], gpre_ref[...], sh_ref[...], sc_ref[...]).astype(BF16)

    u = jnp.dot(h_scr[...], w_ref[...], preferred_element_type=F32)

    def roped():
        return jnp.concatenate(
            [_rope_tile(u[:, s * LANES:(s + 1) * LANES], cos_ref[...], sin_ref[...]) for s in range(AT_TN // LANES)],
            axis=1)

    is_latent = i >= PROMPT_BLKS

    @pl.when(jnp.logical_and(j < q_tiles, is_latent))
    def _():
        q_ref[...] = roped().astype(BF16)

    @pl.when(jnp.logical_and(j < q_tiles, jnp.logical_not(is_latent)))
    def _():
        q_ref[...] = u.astype(BF16)

    @pl.when(jnp.logical_and(j == q_tiles, is_latent))
    def _():
        kv_ref[...] = roped()

    @pl.when(jnp.logical_or(j > q_tiles, jnp.logical_and(j == q_tiles, jnp.logical_not(is_latent))))
    def _():
        kv_ref[...] = u


def _at_in_layer(x, mod, gains, layer, w_qkv, cos, sin):
    q_tiles = AT_Q // AT_TN
    row = pl.BlockSpec((ROW_BLK, D), lambda i, j: (i, 0))
    tab = pl.BlockSpec((ROW_BLK, LANES), lambda i, j: (0, 0))
    return pl.pallas_call(
        _at_in_kernel,
        grid=(N_BLKS, (AT_Q + 2 * AT_KVW) // AT_TN),
        in_specs=[row, _mod_spec(layer, 0), _mod_spec(layer, 1), _gain_spec(layer, 0),
                  pl.BlockSpec((D, AT_TN), lambda i, j: (0, j)), tab, tab],
        out_specs=[
            pl.BlockSpec((ROW_BLK, AT_TN), lambda i, j: (i, jnp.minimum(j, q_tiles - 1))),
            pl.BlockSpec((ROW_BLK, AT_TN), lambda i, j: (i, jnp.maximum(j - q_tiles, 0))),
        ],
        out_shape=[jax.ShapeDtypeStruct((N_TOK, AT_Q), BF16), jax.ShapeDtypeStruct((N_TOK, 2 * AT_KVW), F32)],
        scratch_shapes=[pltpu.VMEM((ROW_BLK, D), BF16)],
        compiler_params=pltpu.CompilerParams(
            dimension_semantics=("arbitrary", "arbitrary"), vmem_limit_bytes=VMEM_LIMIT),
        name="attention_in",
    )(x, mod, mod, gains, w_qkv, cos, sin)


def _softmax_pv(scores, values, sink):
    m = sink
    for s in scores:
        m = jnp.maximum(m, jnp.max(s, axis=-1, keepdims=True))
    denom = jnp.exp(sink - m)
    acc = None
    for s, v in zip(scores, values):
        p = jnp.exp(s - m)
        denom = denom + jnp.sum(p, axis=-1, keepdims=True)
        pv = jnp.dot(p.astype(BF16), v, preferred_element_type=F32)
        acc = pv if acc is None else acc + pv
    return acc / denom


def _ctx_attn_kernel(q_ref, kv_ref, sink_ref, y_ref):
    scale = HD ** -0.5
    outs = []
    for kvh in range(AT_KV):
        k = kv_ref[:, kvh * HD:(kvh + 1) * HD].astype(BF16)
        v = kv_ref[:, AT_KVW + kvh * HD:AT_KVW + (kvh + 1) * HD].astype(BF16)
        for g in range(AT_G):
            hh = kvh * AT_G + g
            s = _bdot_nt(q_ref[:, hh * HD:(hh + 1) * HD], k) * scale
            outs.append(_softmax_pv([s], [v], sink_ref[0:1, hh:hh + 1]))
    y_ref[...] = jnp.concatenate(outs, axis=1).astype(BF16)


def _ctx_attention(q, kv, sink):
    nq = L_PROMPT // QBLK
    return pl.pallas_call(
        _ctx_attn_kernel,
        grid=(N_PROMPT, nq),
        in_specs=[
            pl.BlockSpec((QBLK, AT_Q), lambda b, t: (b * nq + t, 0)),
            pl.BlockSpec((L_PROMPT, 2 * AT_KVW), lambda b, t: (b, 0)),
            pl.BlockSpec((1, LANES), lambda b, t: (0, 0)),
        ],
        out_specs=pl.BlockSpec((QBLK, AT_Q), lambda b, t: (b * nq + t, 0)),
        out_shape=jax.ShapeDtypeStruct((TOK_PROMPT, AT_Q), BF16),
        compiler_params=pltpu.CompilerParams(dimension_semantics=("arbitrary", "arbitrary")),
        name="context_attention",
    )(q, kv, sink)


def _lat_attn_kernel(q_ref, kv_ref, ck_ref, cv_ref, sink_ref, y_ref):
    scale = HD ** -0.5
    t = pl.program_id(1)
    k0 = pl.multiple_of(jnp.clip(t * QBLK - WINDOW, 0, L_LATENT - BAND), QBLK)
    qpos = t * QBLK + lax.broadcasted_iota(jnp.int32, (QBLK, BAND), 0)
    kpos = k0 + lax.broadcasted_iota(jnp.int32, (QBLK, BAND), 1)
    valid = jnp.abs(qpos - kpos) <= WINDOW
    outs = []
    for kvh in range(AT_KV):
        k_loc = kv_ref[pl.ds(k0, BAND), kvh * HD:(kvh + 1) * HD].astype(BF16)
        v_loc = kv_ref[pl.ds(k0, BAND), AT_KVW + kvh * HD:AT_KVW + (kvh + 1) * HD].astype(BF16)
        k_ctx = ck_ref[:, kvh * HD:(kvh + 1) * HD].astype(BF16)
        v_ctx = cv_ref[:, kvh * HD:(kvh + 1) * HD].astype(BF16)
        for g in range(AT_G):
            hh = kvh * AT_G + g
            qh = q_ref[:, hh * HD:(hh + 1) * HD]
            s_loc = jnp.where(valid, _bdot_nt(qh, k_loc) * scale, -jnp.inf)
            s_ctx = _bdot_nt(qh, k_ctx) * scale
            outs.append(_softmax_pv([s_loc, s_ctx], [v_loc, v_ctx], sink_ref[0:1, hh:hh + 1]))
    y_ref[...] = jnp.concatenate(outs, axis=1).astype(BF16)


def _lat_attention(q, kv, cache_k, cache_v, sink, attn_layer):
    nq = L_LATENT // QBLK
    q0 = TOK_PROMPT // QBLK
    s0 = TOK_PROMPT // L_LATENT
    cache = pl.BlockSpec((None, None, PAST, AT_KVW), lambda b, t: (b, attn_layer, 0, 0))
    return pl.pallas_call(
        _lat_attn_kernel,
        grid=(N_LATENT, nq),
        in_specs=[
            pl.BlockSpec((QBLK, AT_Q), lambda b, t: (q0 + b * nq + t, 0)),
            pl.BlockSpec((L_LATENT, 2 * AT_KVW), lambda b, t: (s0 + b, 0)),
            cache, cache,
            pl.BlockSpec((1, LANES), lambda b, t: (0, 0)),
        ],
        out_specs=pl.BlockSpec((QBLK, AT_Q), lambda b, t: (b * nq + t, 0)),
        out_shape=jax.ShapeDtypeStruct((N_LATENT * L_LATENT, AT_Q), BF16),
        compiler_params=pltpu.CompilerParams(dimension_semantics=("arbitrary", "arbitrary")),
        name="latent_attention",
    )(q, kv, cache_k, cache_v, sink)


def _rope_tables():
    pos = jnp.arange(L_LATENT)
    half = HD // 2
    inv = jnp.power(ROPE_BASE, -jnp.arange(0, half, 2, dtype=F32) / half)
    ang_row = (pos // GRID_W).astype(F32)[:, None] * inv
    ang_col = (pos % GRID_W).astype(F32)[:, None] * inv
    ang = jnp.concatenate([ang_row, ang_row, ang_col, ang_col], axis=1)
    sign = jnp.tile(jnp.concatenate([-jnp.ones((16,), F32), jnp.ones((16,), F32)]), 2)
    cos = jnp.tile(jnp.cos(ang), (1, 2))
    sin = jnp.tile(jnp.sin(ang) * sign, (1, 2))
    return cos, sin


def _pad_lanes(v):
    v = v.reshape(1, -1).astype(F32)
    return jnp.pad(v, ((0, 0), (0, LANES - v.shape[1])))


def kernel(x_prompt, x_sample, state_delta, cache_k, cache_v, c, c_ctx, w_ada, b_ada, norm_g, dn_w_in, dn_conv_w, dn_a_log, dn_dt_bias, dn_onorm_g, dn_w_out, at_w_qkv, at_sink, at_w_o, ffn_w_up, ffn_conv_w, ffn_conv_b, ffn_w_down):
    x = jnp.concatenate([x_prompt.reshape(TOK_PROMPT, D), x_sample.reshape(N_LATENT * L_LATENT, D)], axis=0)
    cvecs = jnp.concatenate([c_ctx[None, :], c, jnp.zeros((MOD_ROWS - 1 - N_LATENT, D), F32)], axis=0)
    mod = _modulation(cvecs, w_ada, b_ada).reshape(DEPTH, MOD_ROWS, 1, 6 * D)
    gains = norm_g.reshape(DEPTH, 4, 1, D)
    cos, sin = _rope_tables()
    ck = cache_k.reshape(N_LATENT, -1, PAST, AT_KVW)
    cv = cache_v.reshape(N_LATENT, -1, PAST, AT_KVW)
    zero_state = jnp.zeros((2, HEADS_DN, DK, DK), F32)

    states, new_k, new_v = [], [], []
    for layer in range(DEPTH):
        j = layer // 2
        if layer % 2 == 0:
            w_in = dn_w_in[j]
            w_main = w_in[:, :DN_MAIN].astype(BF16)
            w_ab = jnp.pad(w_in[:, DN_MAIN:], ((0, 0), (0, LANES - 4 * HEADS_DN))).astype(BF16)
            conv_w = jnp.pad(dn_conv_w[j], ((0, 0), (0, D)))
            p, gates = _dn_in_layer(x, mod, gains, layer, w_main, conv_w, w_ab,
                                    _pad_lanes(dn_a_log[j]), _pad_lanes(dn_dt_bias[j]))
            onorm = dn_onorm_g[j].reshape(1, DK)
            y_p, s_fin = _dn_scan(p, gates, zero_state, lambda b, h: (0, h, 0, 0), onorm,
                                  n_seq=N_PROMPT, seq_len=L_PROMPT, row_offset=0, with_final=True)
            (y_s,) = _dn_scan(p, gates, state_delta, lambda b, h, j=j: (b, j, 0, h, 0, 0), onorm,
                              n_seq=N_LATENT, seq_len=L_LATENT, row_offset=TOK_PROMPT, with_final=False)
            states.append(s_fin)
            w_o = dn_w_out[j]
        else:
            q, kv = _at_in_layer(x, mod, gains, layer, at_w_qkv[j].astype(BF16), cos, sin)
            sink = _pad_lanes(at_sink[j])
            y_p = _ctx_attention(q, kv, sink)
            y_s = _lat_attention(q, kv, ck, cv, sink, j)
            new_k.append(kv[:TOK_PROMPT, :AT_KVW].reshape(N_PROMPT, L_PROMPT, AT_KV, HD))
            new_v.append(kv[:TOK_PROMPT, AT_KVW:].reshape(N_PROMPT, L_PROMPT, AT_KV, HD))
            w_o = at_w_o[j]
        y = jnp.concatenate([y_p, y_s], axis=0)
        x = _out_layer(y, w_o.astype(BF16), x, mod, gains, layer)
        x = _ffn_layer(x, mod, gains, layer, ffn_w_up[layer].astype(BF16), ffn_conv_w[layer],
                       ffn_conv_b[layer].reshape(1, 2 * FFN), ffn_w_down[layer].astype(BF16))

    y_prompt = x[:TOK_PROMPT].reshape(N_PROMPT, L_PROMPT, D)
    y_sample = x[TOK_PROMPT:].reshape(N_LATENT, L_LATENT, D)
    return (y_prompt, y_sample, jnp.stack(states, axis=1), jnp.stack(new_k, axis=1), jnp.stack(new_v, axis=1))
```

```python
import functools

import jax
import jax.numpy as jnp
from jax import lax
from jax.experimental import pallas as pl
from jax.experimental.pallas import tpu as pltpu

F32 = jnp.float32
BF16 = jnp.bfloat16

D = 1024
N_PROMPT, L_PROMPT = 16, 256
N_LATENT, L_LATENT = 8, 1024
PAST = 512
GRID_W = 64
DEPTH = 4
HEADS_DN, DK = 8, 128
CHUNK = 64
DN_MAIN = 4 * D
AT_HEADS, AT_KV, AT_G, HD = 16, 4, 4, 64
AT_Q = AT_HEADS * HD
AT_KVW = AT_KV * HD
WINDOW = 128
QBLK = 128
BAND = QBLK + 2 * WINDOW
FFN = 2816
EPS = 1e-6
ROPE_BASE = 10000.0

ROW_BLK = 1024
TOK_PROMPT = N_PROMPT * L_PROMPT
N_TOK = TOK_PROMPT + N_LATENT * L_LATENT
PROMPT_BLKS = TOK_PROMPT // ROW_BLK
N_BLKS = N_TOK // ROW_BLK
MOD_ROWS = 16
LANES = 128
FFN_TN = 256
DN_TN = 512
AT_TN = 256
VMEM_LIMIT = 56 * 1024 * 1024


def _mod_row(i):
    return jnp.where(i < PROMPT_BLKS, 0, i - (PROMPT_BLKS - 1))


def _mod_spec(layer, chunk):
    return pl.BlockSpec((None, None, 1, D), lambda i, *_: (layer, _mod_row(i), 0, chunk))


def _gain_spec(layer, k):
    return pl.BlockSpec((None, None, 1, D), lambda i, *_: (layer, k, 0, 0))


def _rms(x, g):
    return x * lax.rsqrt(jnp.mean(x * x, axis=-1, keepdims=True) + EPS) * g


def _pre(x, g, shift, scale):
    return _rms(x, g) * (1.0 + scale) + shift


def _silu(x):
    return x * jax.nn.sigmoid(x)


SEG = L_PROMPT
N_SEG = ROW_BLK // SEG
HALO_ROWS = 16


def _halo_rows(h, i):
    inside = jnp.where(i < PROMPT_BLKS, 0.0, 1.0)
    picks = []
    for s in range(1, N_SEG):
        picks += [h[s * SEG - 1:s * SEG], h[s * SEG:s * SEG + 1]]
    picks.append(jnp.zeros((HALO_ROWS - len(picks), h.shape[1]), F32))
    return (jnp.concatenate(picks, axis=0) * inside).astype(BF16)


def _conv3(u, w, halo, s):
    sub = lax.broadcasted_iota(jnp.int32, (8, 1), 0)
    zero_row = jnp.zeros((1, u.shape[1]), F32)
    before = halo[2 * s - 2:2 * s - 1] if s > 0 else zero_row
    after = halo[2 * s + 1:2 * s + 2] if s < N_SEG - 1 else zero_row
    down = pltpu.roll(u, 1, 0)
    up = pltpu.roll(u, SEG - 1, 0)
    prev = jnp.concatenate([jnp.where(sub == 0, before, down[:8]), down[8:]], axis=0)
    nxt = jnp.concatenate([up[:SEG - 8], jnp.where(sub == 7, after, up[SEG - 8:])], axis=0)
    return prev * w[0:1] + u * w[1:2] + nxt * w[2:3]


def _segments_pipelined(project, consume):
    pending = project(0)
    for s in range(N_SEG):
        upcoming = project(s + 1) if s + 1 < N_SEG else None
        consume(s, pending)
        pending = upcoming


def _mod_kernel(c_ref, w_ref, b_ref, o_ref):
    s = _silu(c_ref[...]).astype(BF16)
    o_ref[...] = jnp.dot(s, w_ref[...].astype(BF16), preferred_element_type=F32) + b_ref[...]


def _modulation(cvecs, w_ada, b_ada):
    tn = 1536
    return pl.pallas_call(
        _mod_kernel,
        grid=(DEPTH, 6 * D // tn),
        in_specs=[
            pl.BlockSpec((MOD_ROWS, D), lambda l, j: (0, 0)),
            pl.BlockSpec((None, D, tn), lambda l, j: (l, 0, j)),
            pl.BlockSpec((None, 1, tn), lambda l, j: (l, 0, j)),
        ],
        out_specs=pl.BlockSpec((None, MOD_ROWS, tn), lambda l, j: (l, 0, j)),
        out_shape=jax.ShapeDtypeStruct((DEPTH, MOD_ROWS, 6 * D), F32),
        compiler_params=pltpu.CompilerParams(
            dimension_semantics=("arbitrary", "arbitrary"), vmem_limit_bytes=VMEM_LIMIT),
        name="modulation",
    )(cvecs, w_ada, b_ada.reshape(DEPTH, 1, 6 * D))


def _ffn_kernel(x_ref, sh_ref, sc_ref, gt_ref, gpre_ref, gpost_ref, wa_ref, wb_ref, cwa_ref, cwb_ref,
                cba_ref, cbb_ref, wd_ref, o_ref, h_scr, halo_scr, acc_scr):
    i, j = pl.program_id(0), pl.program_id(1)

    @pl.when(j == 0)
    def _():
        h = _pre(x_ref[...], gpre_ref[...], sh_ref[...], sc_ref[...])
        h_scr[...] = h.astype(BF16)
        halo_scr[...] = _halo_rows(h, i)
        acc_scr[...] = jnp.zeros_like(acc_scr)

    halo_a = jnp.dot(halo_scr[...], wa_ref[...], preferred_element_type=F32)
    halo_b = jnp.dot(halo_scr[...], wb_ref[...], preferred_element_type=F32)

    def project(s):
        h = h_scr[s * SEG:(s + 1) * SEG, :]
        return (jnp.dot(h, wa_ref[...], preferred_element_type=F32), jnp.dot(h, wb_ref[...], preferred_element_type=F32))

    def consume(s, up):
        a = _conv3(up[0], cwa_ref[...], halo_a, s) + cba_ref[...]
        b = _conv3(up[1], cwb_ref[...], halo_b, s) + cbb_ref[...]
        act = (_silu(a) * b).astype(BF16)
        acc_scr[s * SEG:(s + 1) * SEG, :] += jnp.dot(act, wd_ref[...], preferred_element_type=F32)

    _segments_pipelined(project, consume)

    @pl.when(j == pl.num_programs(1) - 1)
    def _():
        o_ref[...] = x_ref[...] + gt_ref[...] * _rms(acc_scr[...], gpost_ref[...])


def _ffn_layer(x, mod, gains, layer, w_up, conv_w, conv_b, w_down):
    nj = FFN // FFN_TN
    row = pl.BlockSpec((ROW_BLK, D), lambda i, j: (i, 0))
    return pl.pallas_call(
        _ffn_kernel,
        grid=(N_BLKS, nj),
        in_specs=[
            row, _mod_spec(layer, 3), _mod_spec(layer, 4), _mod_spec(layer, 5),
            _gain_spec(layer, 2), _gain_spec(layer, 3),
            pl.BlockSpec((D, FFN_TN), lambda i, j: (0, j)),
            pl.BlockSpec((D, FFN_TN), lambda i, j: (0, nj + j)),
            pl.BlockSpec((3, FFN_TN), lambda i, j: (0, j)),
            pl.BlockSpec((3, FFN_TN), lambda i, j: (0, nj + j)),
            pl.BlockSpec((1, FFN_TN), lambda i, j: (0, j)),
            pl.BlockSpec((1, FFN_TN), lambda i, j: (0, nj + j)),
            pl.BlockSpec((FFN_TN, D), lambda i, j: (j, 0)),
        ],
        out_specs=row,
        out_shape=jax.ShapeDtypeStruct((N_TOK, D), F32),
        scratch_shapes=[pltpu.VMEM((ROW_BLK, D), BF16), pltpu.VMEM((HALO_ROWS, D), BF16),
                        pltpu.VMEM((ROW_BLK, D), F32)],
        compiler_params=pltpu.CompilerParams(
            dimension_semantics=("arbitrary", "arbitrary"), vmem_limit_bytes=VMEM_LIMIT),
        name="conv_ffn",
    )(x, mod, mod, mod, gains, gains, w_up, w_up, conv_w, conv_w, conv_b, conv_b, w_down)


def _out_kernel(yp_ref, ys_ref, w_ref, x_ref, gt_ref, g_ref, o_ref):
    mixed = jnp.where(pl.program_id(0) < PROMPT_BLKS, yp_ref[...], ys_ref[...])
    y = jnp.dot(mixed, w_ref[...], preferred_element_type=F32)
    o_ref[...] = x_ref[...] + gt_ref[...] * _rms(y, g_ref[...])


def _out_layer(y_prompt, y_latent, w, x, mod, gains, layer):
    row = pl.BlockSpec((ROW_BLK, D), lambda i: (i, 0))
    return pl.pallas_call(
        _out_kernel,
        grid=(N_BLKS,),
        in_specs=[
            pl.BlockSpec((ROW_BLK, D), lambda i: (jnp.minimum(i, PROMPT_BLKS - 1), 0)),
            pl.BlockSpec((ROW_BLK, D), lambda i: (jnp.maximum(i - PROMPT_BLKS, 0), 0)),
            pl.BlockSpec((D, D), lambda i: (0, 0)), row, _mod_spec(layer, 2), _gain_spec(layer, 1)],
        out_specs=row,
        out_shape=jax.ShapeDtypeStruct((N_TOK, D), F32),
        compiler_params=pltpu.CompilerParams(
            dimension_semantics=("arbitrary",), vmem_limit_bytes=VMEM_LIMIT),
        name="mixer_out",
    )(y_prompt, y_latent, w, x, mod, gains)


def _dn_in_kernel(x_ref, sh_ref, sc_ref, gpre_ref, w_ref, cw_ref, wab_ref, alog_ref, dtb_ref,
                  p_ref, g_ref, h_scr, halo_scr):
    i, j = pl.program_id(0), pl.program_id(1)

    @pl.when(j == 0)
    def _():
        hf = _pre(x_ref[...], gpre_ref[...], sh_ref[...], sc_ref[...])
        h = hf.astype(BF16)
        h_scr[...] = h
        halo_scr[...] = _halo_rows(hf, i)
        ab = jnp.dot(h, wab_ref[...], preferred_element_type=F32)
        t = ab + dtb_ref[...]
        softplus = jnp.maximum(t, 0.0) + jnp.log(1.0 + jnp.exp(-jnp.abs(t)))
        lane = lax.broadcasted_iota(jnp.int32, ab.shape, 1)
        g_ref[...] = jnp.where(lane < 2 * HEADS_DN, -jnp.exp(alog_ref[...]) * softplus, jax.nn.sigmoid(ab))

    tiles_per_part = D // DN_TN

    def project(s):
        return jnp.dot(h_scr[s * SEG:(s + 1) * SEG, :], w_ref[...], preferred_element_type=F32)

    @pl.when(j < 2 * tiles_per_part)
    def _():
        halo = jnp.dot(halo_scr[...], w_ref[...], preferred_element_type=F32)
        qscale = jnp.where(j < tiles_per_part, DK ** -0.5, 1.0)

        def consume(s, u):
            act = _silu(_conv3(u, cw_ref[...], halo, s))
            for t in range(DN_TN // DK):
                xs = act[:, t * DK:(t + 1) * DK]
                inv = lax.rsqrt(jnp.sum(xs * xs, axis=-1, keepdims=True) + EPS) * qscale
                p_ref[s * SEG:(s + 1) * SEG, t * DK:(t + 1) * DK] = (xs * inv).astype(BF16)

        _segments_pipelined(project, consume)

    @pl.when(jnp.logical_and(j >= 2 * tiles_per_part, j < 3 * tiles_per_part))
    def _():
        halo = jnp.dot(halo_scr[...], w_ref[...], preferred_element_type=F32)

        def consume(s, u):
            p_ref[s * SEG:(s + 1) * SEG, :] = _silu(_conv3(u, cw_ref[...], halo, s)).astype(BF16)

        _segments_pipelined(project, consume)

    @pl.when(j >= 3 * tiles_per_part)
    def _():
        p_ref[...] = jnp.dot(h_scr[...], w_ref[...], preferred_element_type=F32).astype(BF16)


def _dn_in_layer(x, mod, gains, layer, w_main, conv_w, w_ab, a_log, dt_bias):
    row = pl.BlockSpec((ROW_BLK, D), lambda i, j: (i, 0))
    vec = pl.BlockSpec((1, LANES), lambda i, j: (0, 0))
    return pl.pallas_call(
        _dn_in_kernel,
        grid=(N_BLKS, DN_MAIN // DN_TN),
        in_specs=[
            row, _mod_spec(layer, 0), _mod_spec(layer, 1), _gain_spec(layer, 0),
            pl.BlockSpec((D, DN_TN), lambda i, j: (0, j)),
            pl.BlockSpec((3, DN_TN), lambda i, j: (0, j)),
            pl.BlockSpec((D, LANES), lambda i, j: (0, 0)),
            vec, vec,
        ],
        out_specs=[
            pl.BlockSpec((ROW_BLK, DN_TN), lambda i, j: (i, j)),
            pl.BlockSpec((ROW_BLK, LANES), lambda i, j: (i, 0)),
        ],
        out_shape=[jax.ShapeDtypeStruct((N_TOK, DN_MAIN), BF16), jax.ShapeDtypeStruct((N_TOK, LANES), F32)],
        scratch_shapes=[pltpu.VMEM((ROW_BLK, D), BF16), pltpu.VMEM((HALO_ROWS, D), BF16)],
        compiler_params=pltpu.CompilerParams(
            dimension_semantics=("arbitrary", "arbitrary"), vmem_limit_bytes=VMEM_LIMIT),
        name="deltanet_in",
    )(x, mod, mod, gains, w_main, conv_w, w_ab, a_log, dt_bias)


def _bdot_nt(a, b):
    return lax.dot_general(a.astype(BF16), b.astype(BF16), (((1,), (1,)), ((), ())), preferred_element_type=F32)


def _bdot_tn(a, b):
    return lax.dot_general(a.astype(BF16), b.astype(BF16), (((0,), (0,)), ((), ())), preferred_element_type=F32)


def _bmm(a, b):
    return jnp.einsum('gmk,gkn->gmn', a.astype(BF16), b.astype(BF16), preferred_element_type=F32)


def _bmm_nt(a, b):
    return jnp.einsum('gmk,gnk->gmn', a.astype(BF16), b.astype(BF16), preferred_element_type=F32)


SOLVE_BLK = 16


def _invert_diagonal_blocks(tri, reverse):
    G, C, nb = tri.shape[0], CHUNK, CHUNK // SOLVE_BLK
    r = lax.broadcasted_iota(jnp.int32, (C, C), 0)
    c = lax.broadcasted_iota(jnp.int32, (C, C), 1)
    diag = jnp.where(r // SOLVE_BLK == c // SOLVE_BLK, tri, jnp.zeros_like(tri))
    rr = lax.broadcasted_iota(jnp.int32, (C, SOLVE_BLK * LANES), 0)
    cc = lax.broadcasted_iota(jnp.int32, (C, SOLVE_BLK * LANES), 1)
    pick = jnp.where(rr % SOLVE_BLK == cc // LANES, 1.0, 0.0).astype(BF16)
    cols = jnp.einsum('gik,kn->gin', diag, pick, preferred_element_type=F32)
    inv_d = jnp.broadcast_to(jnp.where(r == c, 1.0, 0.0), (G, C, C))
    steps = range(SOLVE_BLK - 1, 0, -1) if reverse else range(SOLVE_BLK - 1)
    for s in steps:
        pivot_rows = jnp.concatenate(
            [jnp.broadcast_to(inv_d[:, b * SOLVE_BLK + s:b * SOLVE_BLK + s + 1, :], (G, SOLVE_BLK, C))
             for b in range(nb)], axis=1)
        inv_d = inv_d - cols[:, :, s * LANES:s * LANES + C] * pivot_rows
    return inv_d.astype(BF16)


def _block_substitution(systems):
    C, nb = CHUNK, CHUNK // SOLVE_BLK
    r = lax.broadcasted_iota(jnp.int32, (C, C), 0)
    c = lax.broadcasted_iota(jnp.int32, (C, C), 1)
    offs = [jnp.where(r // SOLVE_BLK != c // SOLVE_BLK, tri, jnp.zeros_like(tri)) for tri, _, _, _ in systems]

    def placed(block, b):
        G, _, width = block.shape
        parts = []
        if b > 0:
            parts.append(jnp.zeros((G, b * SOLVE_BLK, width), F32))
        parts.append(block)
        if b < nb - 1:
            parts.append(jnp.zeros((G, (nb - 1 - b) * SOLVE_BLK, width), F32))
        return jnp.concatenate(parts, axis=1)

    xs = [None] * len(systems)
    for step in range(nb):
        ys = []
        for n, (_, _, rhs, reverse) in enumerate(systems):
            b = nb - 1 - step if reverse else step
            rows = slice(b * SOLVE_BLK, (b + 1) * SOLVE_BLK)
            ys.append(rhs[:, rows] if xs[n] is None else rhs[:, rows] - _bmm(offs[n][:, rows], xs[n]))
        for n, (_, inv_d, _, reverse) in enumerate(systems):
            b = nb - 1 - step if reverse else step
            rows = slice(b * SOLVE_BLK, (b + 1) * SOLVE_BLK)
            xb = placed(_bmm(inv_d[:, rows], placed(ys[n], b)), b)
            xs[n] = xb if xs[n] is None else xs[n] + xb
    return xs


def _dn_chunk_terms(q, k, v, g_raw, beta, reverse):
    G, C = q.shape[0], CHUNK
    r = lax.broadcasted_iota(jnp.int32, (C, C), 0)
    c = lax.broadcasted_iota(jnp.int32, (C, C), 1)
    incl = (r <= c) if reverse else (r >= c)
    strict = (r < c) if reverse else (r > c)
    g_mat = jnp.broadcast_to(g_raw, (G, C, C))
    scanned = (r >= c) if reverse else (r <= c)
    cum_row = jnp.sum(jnp.where(scanned, g_mat, 0.0), axis=1, keepdims=True)
    cum_col = jnp.sum(jnp.where(r == c, jnp.broadcast_to(cum_row, (G, C, C)), 0.0), axis=2, keepdims=True)
    decay = jnp.exp(jnp.where(incl, cum_col - cum_row, -jnp.inf))
    kb = k * beta
    kk = _bmm_nt(kb, k) * decay
    qk = _bmm_nt(q, k) * decay
    tri = jnp.where(strict, kk, 0.0).astype(BF16)
    e_col = jnp.exp(cum_col)
    rhs = jnp.concatenate([v * beta, kb * e_col], axis=2)
    total = cum_col[:, 0:1] if reverse else cum_col[:, C - 1:C]
    k_end = k * jnp.exp(total - cum_col)
    return tri, rhs, q * e_col, k_end, qk, jnp.exp(total)


PREP_CHUNKS = 4
HEAD_GRP = 4


def _dn_scan_kernel(q_ref, k_ref, v_ref, z_ref, g_ref, s0_ref, on_ref, y_ref, *rest, seq_len, with_final):
    if with_final:
        sfin_ref, rest = rest[0], rest[1:]
    u_scr, w_scr, qe_scr, ke_scr, qk_scr, el_scr, o_scr, s_scr = rest
    n = seq_len // CHUNK
    G, HG = PREP_CHUNKS, HEAD_GRP
    head0 = pl.program_id(1) * HG
    lane = lax.broadcasted_iota(jnp.int32, (G, CHUNK, LANES), 2)

    def prepare(grp, carry):
        rows = pl.ds(pl.multiple_of(grp * (G * CHUNK), G * CHUNK), G * CHUNK)

        def by_head(ref):
            x = ref[rows, :].astype(F32)
            return jnp.concatenate([x[:, hh * DK:(hh + 1) * DK].reshape(G, CHUNK, DK) for hh in range(HG)], axis=0)

        q, k, v = by_head(q_ref), by_head(k_ref), by_head(v_ref)
        gates = g_ref[rows, :].reshape(G, CHUNK, LANES)

        def gate(col):
            return jnp.concatenate(
                [jnp.sum(jnp.where(lane == col + head0 + hh, gates, 0.0), axis=2, keepdims=True) for hh in range(HG)],
                axis=0)

        terms = [_dn_chunk_terms(q, k, v, gate(d * HEADS_DN), gate((2 + d) * HEADS_DN), reverse=(d == 1))
                 for d in range(2)]
        inverses = [_invert_diagonal_blocks(terms[d][0], reverse=(d == 1)) for d in range(2)]
        sols = _block_substitution([(terms[d][0], inverses[d], terms[d][1], d == 1) for d in range(2)])
        for d in range(2):
            _, _, qe, ke, qk, el = terms[d]
            for hh in range(HG):
                part = slice(hh * G, (hh + 1) * G)
                u_scr[d, hh, rows, :] = sols[d][part, :, :DK].reshape(G * CHUNK, DK)
                w_scr[d, hh, rows, :] = sols[d][part, :, DK:].reshape(G * CHUNK, DK).astype(BF16)
                qe_scr[d, hh, rows, :] = qe[part].reshape(G * CHUNK, DK).astype(BF16)
                ke_scr[d, hh, rows, :] = ke[part].reshape(G * CHUNK, DK).astype(BF16)
                qk_scr[d, hh, rows, :] = qk[part].reshape(G * CHUNK, CHUNK).astype(BF16)
                el_scr[d, hh, pl.ds(grp * G, G)] = jnp.broadcast_to(el[part], (G, 8, LANES))
        return carry

    lax.fori_loop(0, n // G, prepare, 0)
    s_scr[...] = s0_ref[...]

    def scan(t, carry):
        chains = [(hh, d) for hh in range(HG) for d in range(2)]
        rows = [pl.ds(pl.multiple_of((t if d == 0 else n - 1 - t) * CHUNK, CHUNK), CHUNK) for _, d in chains]
        states = [s_scr[d, hh] for hh, d in chains]
        from_state = []
        for (hh, d), r, s in zip(chains, rows, states):
            sb = s.astype(BF16)
            from_state.append((jnp.dot(w_scr[d, hh, r, :], sb, preferred_element_type=F32),
                               jnp.dot(qe_scr[d, hh, r, :], sb, preferred_element_type=F32)))
        for (hh, d), r, s, (ws, qs) in zip(chains, rows, states, from_state):
            c = t if d == 0 else n - 1 - t
            vb = (u_scr[d, hh, r, :] - ws).astype(BF16)
            o_scr[d, hh, r, :] = qs + jnp.dot(qk_scr[d, hh, r, :], vb, preferred_element_type=F32)
            s_scr[d, hh] = s * el_scr[d, hh, c][0:1, :] + _bdot_tn(ke_scr[d, hh, r, :], vb)
        return carry

    lax.fori_loop(0, n, scan, 0)
    if with_final:
        sfin_ref[...] = s_scr[...]
    for hh in range(HG):
        cols = slice(hh * DK, (hh + 1) * DK)
        o = o_scr[0, hh] + o_scr[1, hh]
        y_ref[:, cols] = (_rms(o, on_ref[...]) * _silu(z_ref[:, cols].astype(F32))).astype(BF16)


def _dn_scan(p, gates, s0, s0_index, onorm_g, *, n_seq, seq_len, row_offset, with_final):
    blk0 = row_offset // seq_len
    HG = HEAD_GRP
    n_grp = HEADS_DN // HG

    def part(k):
        return pl.BlockSpec((seq_len, HG * DK), lambda b, h: (blk0 + b, k * n_grp + h))

    out_specs = [pl.BlockSpec((seq_len, HG * DK), lambda b, h: (b, h))]
    out_shape = [jax.ShapeDtypeStruct((n_seq * seq_len, D), BF16)]
    if with_final:
        out_specs.append(pl.BlockSpec((None, 2, HG, DK, DK), lambda b, h: (b, 0, h, 0, 0)))
        out_shape.append(jax.ShapeDtypeStruct((n_seq, 2, HEADS_DN, DK, DK), F32))
    n = seq_len // CHUNK
    res = pl.pallas_call(
        functools.partial(_dn_scan_kernel, seq_len=seq_len, with_final=with_final),
        grid=(n_seq, n_grp),
        in_specs=[
            part(0), part(1), part(2), part(3),
            pl.BlockSpec((seq_len, LANES), lambda b, h: (blk0 + b, 0)),
            pl.BlockSpec((None,) * (s0.ndim - 4) + (2, HG, DK, DK), s0_index),
            pl.BlockSpec((1, DK), lambda b, h: (0, 0)),
        ],
        out_specs=out_specs,
        out_shape=out_shape,
        scratch_shapes=[
            pltpu.VMEM((2, HG, seq_len, DK), F32),
            pltpu.VMEM((2, HG, seq_len, DK), BF16),
            pltpu.VMEM((2, HG, seq_len, DK), BF16),
            pltpu.VMEM((2, HG, seq_len, DK), BF16),
            pltpu.VMEM((2, HG, seq_len, CHUNK), BF16),
            pltpu.VMEM((2, HG, n, 8, LANES), F32),
            pltpu.VMEM((2, HG, seq_len, DK), F32),
            pltpu.VMEM((2, HG, DK, DK), F32),
        ],
        compiler_params=pltpu.CompilerParams(
            dimension_semantics=("arbitrary", "arbitrary"), vmem_limit_bytes=VMEM_LIMIT),
        name="deltanet_scan_%d" % seq_len,
    )(p, p, p, p, gates, s0, onorm_g)
    return res


def _rope_tile(x, cos, sin):
    lane = lax.broadcasted_iota(jnp.int32, x.shape, 1)
    width = x.shape[1]
    partner = jnp.where((lane & 31) < 16, pltpu.roll(x, width - 16, 1), pltpu.roll(x, 16, 1))
    return x * cos + partner * sin


def _at_in_kernel(x_ref, sh_ref, sc_ref, gpre_ref, w_ref, cos_ref, sin_ref, q_ref, kv_ref, h_scr):
    i, j = pl.program_id(0), pl.program_id(1)
    q_tiles = AT_Q // AT_TN

    @pl.when(j == 0)
    def _():
        h_scr[...] = _pre(x_ref[...], gpre_ref[...], sh_ref[...], sc_ref[...]).astype(BF16)

    u = jnp.dot(h_scr[...], w_ref[...], preferred_element_type=F32)

    def roped():
        return jnp.concatenate(
            [_rope_tile(u[:, s * LANES:(s + 1) * LANES], cos_ref[...], sin_ref[...]) for s in range(AT_TN // LANES)],
            axis=1)

    is_latent = i >= PROMPT_BLKS

    @pl.when(jnp.logical_and(j < q_tiles, is_latent))
    def _():
        q_ref[...] = (roped() * HD ** -0.5).astype(BF16)

    @pl.when(jnp.logical_and(j < q_tiles, jnp.logical_not(is_latent)))
    def _():
        q_ref[...] = (u * HD ** -0.5).astype(BF16)

    @pl.when(jnp.logical_and(j == q_tiles, is_latent))
    def _():
        kv_ref[...] = roped()

    @pl.when(jnp.logical_or(j > q_tiles, jnp.logical_and(j == q_tiles, jnp.logical_not(is_latent))))
    def _():
        kv_ref[...] = u


def _at_in_layer(x, mod, gains, layer, w_qkv, cos, sin):
    q_tiles = AT_Q // AT_TN
    row = pl.BlockSpec((ROW_BLK, D), lambda i, j: (i, 0))
    tab = pl.BlockSpec((ROW_BLK, LANES), lambda i, j: (0, 0))
    return pl.pallas_call(
        _at_in_kernel,
        grid=(N_BLKS, (AT_Q + 2 * AT_KVW) // AT_TN),
        in_specs=[row, _mod_spec(layer, 0), _mod_spec(layer, 1), _gain_spec(layer, 0),
                  pl.BlockSpec((D, AT_TN), lambda i, j: (0, j)), tab, tab],
        out_specs=[
            pl.BlockSpec((ROW_BLK, AT_TN), lambda i, j: (i, jnp.minimum(j, q_tiles - 1))),
            pl.BlockSpec((ROW_BLK, AT_TN), lambda i, j: (i, jnp.maximum(j - q_tiles, 0))),
        ],
        out_shape=[jax.ShapeDtypeStruct((N_TOK, AT_Q), BF16), jax.ShapeDtypeStruct((N_TOK, 2 * AT_KVW), F32)],
        scratch_shapes=[pltpu.VMEM((ROW_BLK, D), BF16)],
        compiler_params=pltpu.CompilerParams(
            dimension_semantics=("arbitrary", "arbitrary"), vmem_limit_bytes=VMEM_LIMIT),
        name="attention_in",
    )(x, mod, mod, gains, w_qkv, cos, sin)


def _softmax_pv(scores, values, sink):
    m = sink
    for s in scores:
        m = jnp.maximum(m, jnp.max(s, axis=-1, keepdims=True))
    denom = jnp.exp(sink - m)
    acc = None
    for s, v in zip(scores, values):
        p = jnp.exp(s - m)
        denom = denom + jnp.sum(p, axis=-1, keepdims=True)
        pv = jnp.dot(p.astype(BF16), v, preferred_element_type=F32)
        acc = pv if acc is None else acc + pv
    return acc / denom


def _attend_groups(q_ref, scores_of, values_of, sink_ref):
    rows = q_ref.shape[0]

    def stacked_q(kvh):
        return jnp.concatenate([q_ref[:, (kvh * AT_G + g) * HD:(kvh * AT_G + g + 1) * HD] for g in range(AT_G)], axis=0)

    def stacked_sink(kvh):
        return jnp.concatenate([jnp.broadcast_to(sink_ref[0:1, kvh * AT_G + g:kvh * AT_G + g + 1], (rows, 1))
                                for g in range(AT_G)], axis=0)

    outs = []
    pending = scores_of(stacked_q(0), 0)
    for kvh in range(AT_KV):
        upcoming = scores_of(stacked_q(kvh + 1), kvh + 1) if kvh + 1 < AT_KV else None
        o = _softmax_pv(pending, values_of(kvh), stacked_sink(kvh))
        outs.extend(o[g * rows:(g + 1) * rows] for g in range(AT_G))
        pending = upcoming
    return jnp.concatenate(outs, axis=1).astype(BF16)


def _ctx_attn_kernel(q_ref, kv_ref, sink_ref, y_ref):
    keys = [kv_ref[:, kvh * HD:(kvh + 1) * HD].astype(BF16) for kvh in range(AT_KV)]
    vals = [kv_ref[:, AT_KVW + kvh * HD:AT_KVW + (kvh + 1) * HD].astype(BF16) for kvh in range(AT_KV)]
    y_ref[...] = _attend_groups(q_ref, lambda q, kvh: [_bdot_nt(q, keys[kvh])], lambda kvh: [vals[kvh]], sink_ref)


def _ctx_attention(q, kv, sink):
    nq = L_PROMPT // QBLK
    return pl.pallas_call(
        _ctx_attn_kernel,
        grid=(N_PROMPT, nq),
        in_specs=[
            pl.BlockSpec((QBLK, AT_Q), lambda b, t: (b * nq + t, 0)),
            pl.BlockSpec((L_PROMPT, 2 * AT_KVW), lambda b, t: (b, 0)),
            pl.BlockSpec((1, LANES), lambda b, t: (0, 0)),
        ],
        out_specs=pl.BlockSpec((QBLK, AT_Q), lambda b, t: (b * nq + t, 0)),
        out_shape=jax.ShapeDtypeStruct((TOK_PROMPT, AT_Q), BF16),
        compiler_params=pltpu.CompilerParams(dimension_semantics=("arbitrary", "arbitrary")),
        name="context_attention",
    )(q, kv, sink)


def _lat_attn_kernel(q_ref, kv_ref, ck_ref, cv_ref, sink_ref, y_ref):
    t = pl.program_id(1)
    k0 = pl.multiple_of(jnp.clip(t * QBLK - WINDOW, 0, L_LATENT - BAND), QBLK)
    qpos = t * QBLK + lax.broadcasted_iota(jnp.int32, (QBLK, BAND), 0)
    kpos = k0 + lax.broadcasted_iota(jnp.int32, (QBLK, BAND), 1)
    valid = jnp.concatenate([jnp.abs(qpos - kpos) <= WINDOW] * AT_G, axis=0)
    band = pl.ds(k0, BAND)
    k_loc = [kv_ref[band, kvh * HD:(kvh + 1) * HD].astype(BF16) for kvh in range(AT_KV)]
    v_loc = [kv_ref[band, AT_KVW + kvh * HD:AT_KVW + (kvh + 1) * HD].astype(BF16) for kvh in range(AT_KV)]
    k_ctx = [ck_ref[:, kvh * HD:(kvh + 1) * HD].astype(BF16) for kvh in range(AT_KV)]
    v_ctx = [cv_ref[:, kvh * HD:(kvh + 1) * HD].astype(BF16) for kvh in range(AT_KV)]

    def scores_of(q, kvh):
        return [jnp.where(valid, _bdot_nt(q, k_loc[kvh]), -jnp.inf), _bdot_nt(q, k_ctx[kvh])]

    y_ref[...] = _attend_groups(q_ref, scores_of, lambda kvh: [v_loc[kvh], v_ctx[kvh]], sink_ref)


def _lat_attention(q, kv, cache_k, cache_v, sink, attn_layer):
    nq = L_LATENT // QBLK
    q0 = TOK_PROMPT // QBLK
    s0 = TOK_PROMPT // L_LATENT
    cache = pl.BlockSpec((None, None, PAST, AT_KVW), lambda b, t: (b, attn_layer, 0, 0))
    return pl.pallas_call(
        _lat_attn_kernel,
        grid=(N_LATENT, nq),
        in_specs=[
            pl.BlockSpec((QBLK, AT_Q), lambda b, t: (q0 + b * nq + t, 0)),
            pl.BlockSpec((L_LATENT, 2 * AT_KVW), lambda b, t: (s0 + b, 0)),
            cache, cache,
            pl.BlockSpec((1, LANES), lambda b, t: (0, 0)),
        ],
        out_specs=pl.BlockSpec((QBLK, AT_Q), lambda b, t: (b * nq + t, 0)),
        out_shape=jax.ShapeDtypeStruct((N_LATENT * L_LATENT, AT_Q), BF16),
        compiler_params=pltpu.CompilerParams(dimension_semantics=("arbitrary", "arbitrary")),
        name="latent_attention",
    )(q, kv, cache_k, cache_v, sink)


def _rope_tables():
    pos = jnp.arange(L_LATENT)
    half = HD // 2
    inv = jnp.power(ROPE_BASE, -jnp.arange(0, half, 2, dtype=F32) / half)
    ang_row = (pos // GRID_W).astype(F32)[:, None] * inv
    ang_col = (pos % GRID_W).astype(F32)[:, None] * inv
    ang = jnp.concatenate([ang_row, ang_row, ang_col, ang_col], axis=1)
    sign = jnp.tile(jnp.concatenate([-jnp.ones((16,), F32), jnp.ones((16,), F32)]), 2)
    cos = jnp.tile(jnp.cos(ang), (1, 2))
    sin = jnp.tile(jnp.sin(ang) * sign, (1, 2))
    return cos, sin


def _pad_lanes(v):
    v = v.reshape(1, -1).astype(F32)
    return jnp.pad(v, ((0, 0), (0, LANES - v.shape[1])))


def kernel(x_prompt, x_sample, state_delta, cache_k, cache_v, c, c_ctx, w_ada, b_ada, norm_g, dn_w_in, dn_conv_w, dn_a_log, dn_dt_bias, dn_onorm_g, dn_w_out, at_w_qkv, at_sink, at_w_o, ffn_w_up, ffn_conv_w, ffn_conv_b, ffn_w_down):
    x = jnp.concatenate([x_prompt.reshape(TOK_PROMPT, D), x_sample.reshape(N_LATENT * L_LATENT, D)], axis=0)
    cvecs = jnp.concatenate([c_ctx[None, :], c, jnp.zeros((MOD_ROWS - 1 - N_LATENT, D), F32)], axis=0)
    mod = _modulation(cvecs, w_ada, b_ada).reshape(DEPTH, MOD_ROWS, 1, 6 * D)
    gains = norm_g.reshape(DEPTH, 4, 1, D)
    cos, sin = _rope_tables()
    ck = cache_k.reshape(N_LATENT, -1, PAST, AT_KVW)
    cv = cache_v.reshape(N_LATENT, -1, PAST, AT_KVW)
    zero_state = jnp.zeros((2, HEADS_DN, DK, DK), F32)

    states, new_k, new_v = [], [], []
    for layer in range(DEPTH):
        j = layer // 2
        if layer % 2 == 0:
            w_in = dn_w_in[j]
            w_main = w_in[:, :DN_MAIN].astype(BF16)
            w_ab = jnp.pad(w_in[:, DN_MAIN:], ((0, 0), (0, LANES - 4 * HEADS_DN))).astype(BF16)
            conv_w = jnp.pad(dn_conv_w[j], ((0, 0), (0, D)))
            p, gates = _dn_in_layer(x, mod, gains, layer, w_main, conv_w, w_ab,
                                    _pad_lanes(dn_a_log[j]), _pad_lanes(dn_dt_bias[j]))
            onorm = dn_onorm_g[j].reshape(1, DK)
            y_p, s_fin = _dn_scan(p, gates, zero_state, lambda b, h: (0, h, 0, 0), onorm,
                                  n_seq=N_PROMPT, seq_len=L_PROMPT, row_offset=0, with_final=True)
            (y_s,) = _dn_scan(p, gates, state_delta, lambda b, h, j=j: (b, j, 0, h, 0, 0), onorm,
                              n_seq=N_LATENT, seq_len=L_LATENT, row_offset=TOK_PROMPT, with_final=False)
            states.append(s_fin)
            w_o = dn_w_out[j]
        else:
            q, kv = _at_in_layer(x, mod, gains, layer, at_w_qkv[j].astype(BF16), cos, sin)
            sink = _pad_lanes(at_sink[j])
            y_p = _ctx_attention(q, kv, sink)
            y_s = _lat_attention(q, kv, ck, cv, sink, j)
            new_k.append(kv[:TOK_PROMPT, :AT_KVW].reshape(N_PROMPT, L_PROMPT, AT_KV, HD))
            new_v.append(kv[:TOK_PROMPT, AT_KVW:].reshape(N_PROMPT, L_PROMPT, AT_KV, HD))
            w_o = at_w_o[j]
        x = _out_layer(y_p, y_s, w_o.astype(BF16), x, mod, gains, layer)
        x = _ffn_layer(x, mod, gains, layer, ffn_w_up[layer].astype(BF16), ffn_conv_w[layer],
                       ffn_conv_b[layer].reshape(1, 2 * FFN), ffn_w_down[layer].astype(BF16))

    y_prompt = x[:TOK_PROMPT].reshape(N_PROMPT, L_PROMPT, D)
    y_sample = x[TOK_PROMPT:].reshape(N_LATENT, L_LATENT, D)
    return (y_prompt, y_sample, jnp.stack(states, axis=1), jnp.stack(new_k, axis=1), jnp.stack(new_v, axis=1))
```

```python
import functools

import jax
import jax.numpy as jnp
from jax import lax
from jax.experimental import pallas as pl
from jax.experimental.pallas import tpu as pltpu

F32 = jnp.float32
BF16 = jnp.bfloat16

D = 1024
N_PROMPT, L_PROMPT = 16, 256
N_LATENT, L_LATENT = 8, 1024
PAST = 512
GRID_W = 64
DEPTH = 4
HEADS_DN, DK = 8, 128
CHUNK = 64
DN_MAIN = 4 * D
AT_HEADS, AT_KV, AT_G, HD = 16, 4, 4, 64
AT_Q = AT_HEADS * HD
AT_KVW = AT_KV * HD
WINDOW = 128
QBLK = 128
BAND = QBLK + 2 * WINDOW
FFN = 2816
EPS = 1e-6
ROPE_BASE = 10000.0

ROW_BLK = 1024
TOK_PROMPT = N_PROMPT * L_PROMPT
N_TOK = TOK_PROMPT + N_LATENT * L_LATENT
PROMPT_BLKS = TOK_PROMPT // ROW_BLK
N_BLKS = N_TOK // ROW_BLK
MOD_ROWS = 16
LANES = 128
FFN_TN = 1408
MXU_N = 256
DN_TN = 512
AT_TN = 256
VMEM_LIMIT = 56 * 1024 * 1024


def _mod_row(i):
    return jnp.where(i < PROMPT_BLKS, 0, i - (PROMPT_BLKS - 1))


def _mod_spec(layer, chunk):
    return pl.BlockSpec((None, None, 1, D), lambda i, *_: (layer, _mod_row(i), 0, chunk))


def _gain_spec(layer, k):
    return pl.BlockSpec((None, None, 1, D), lambda i, *_: (layer, k, 0, 0))


def _split_specs(width):
    return [pl.BlockSpec((ROW_BLK, width), lambda i, *_: (jnp.minimum(i, PROMPT_BLKS - 1), 0)),
            pl.BlockSpec((ROW_BLK, width), lambda i, *_: (jnp.maximum(i - PROMPT_BLKS, 0), 0))]


def _pick(i, prompt_ref, latent_ref):
    return jnp.where(i < PROMPT_BLKS, prompt_ref[...], latent_ref[...])


def _rms(x, g):
    return x * lax.rsqrt(jnp.mean(x * x, axis=-1, keepdims=True) + EPS) * g


def _pre(x, g, shift, scale):
    return _rms(x, g) * (1.0 + scale) + shift


def _silu(x):
    return x * jax.nn.sigmoid(x)


SEG = L_PROMPT
N_SEG = ROW_BLK // SEG
HALO_ROWS = 16


def _halo_rows(h, i):
    inside = jnp.where(i < PROMPT_BLKS, 0.0, 1.0)
    picks = []
    for s in range(1, N_SEG):
        picks += [h[s * SEG - 1:s * SEG], h[s * SEG:s * SEG + 1]]
    picks.append(jnp.zeros((HALO_ROWS - len(picks), h.shape[1]), F32))
    return (jnp.concatenate(picks, axis=0) * inside).astype(BF16)


def _conv3(u, w, halo, s):
    sub = lax.broadcasted_iota(jnp.int32, (8, 1), 0)
    zero_row = jnp.zeros((1, u.shape[1]), F32)
    before = halo[2 * s - 2:2 * s - 1] if s > 0 else zero_row
    after = halo[2 * s + 1:2 * s + 2] if s < N_SEG - 1 else zero_row
    down = pltpu.roll(u, 1, 0)
    up = pltpu.roll(u, SEG - 1, 0)
    prev = jnp.concatenate([jnp.where(sub == 0, before, down[:8]), down[8:]], axis=0)
    nxt = jnp.concatenate([up[:SEG - 8], jnp.where(sub == 7, after, up[SEG - 8:])], axis=0)
    return prev * w[0:1] + u * w[1:2] + nxt * w[2:3]


def _pipelined(stages, project, consume, ahead=1):
    pending = [project(st) for st in stages[:ahead]]
    for n, stage in enumerate(stages):
        if n + ahead < len(stages):
            pending.append(project(stages[n + ahead]))
        consume(stage, pending.pop(0))


def _segments_pipelined(project, consume):
    _pipelined(list(range(N_SEG)), project, consume)


def _sub_tiles(width):
    tiles, off = [], 0
    while off < width:
        size = min(MXU_N, width - off)
        tiles.append((off, size))
        off += size
    return tiles


def _mod_kernel(c_ref, w_ref, b_ref, o_ref):
    s = _silu(c_ref[...]).astype(BF16)
    o_ref[...] = jnp.dot(s, w_ref[...].astype(BF16), preferred_element_type=F32) + b_ref[...]


def _modulation(cvecs, w_ada, b_ada):
    tn = 1536
    return pl.pallas_call(
        _mod_kernel,
        grid=(DEPTH, 6 * D // tn),
        in_specs=[
            pl.BlockSpec((MOD_ROWS, D), lambda l, j: (0, 0)),
            pl.BlockSpec((None, D, tn), lambda l, j: (l, 0, j)),
            pl.BlockSpec((None, 1, tn), lambda l, j: (l, 0, j)),
        ],
        out_specs=pl.BlockSpec((None, MOD_ROWS, tn), lambda l, j: (l, 0, j)),
        out_shape=jax.ShapeDtypeStruct((DEPTH, MOD_ROWS, 6 * D), F32),
        compiler_params=pltpu.CompilerParams(
            dimension_semantics=("arbitrary", "arbitrary"), vmem_limit_bytes=VMEM_LIMIT),
        name="modulation",
    )(cvecs, w_ada, b_ada.reshape(DEPTH, 1, 6 * D))


def _ffn_kernel(x_ref, sh_ref, sc_ref, gt_ref, gpre_ref, gpost_ref, wa_ref, wb_ref, cwa_ref, cwb_ref,
                cba_ref, cbb_ref, wd_ref, *rest, split_out):
    outs, (h_scr, halo_scr, acc_scr) = rest[:-3], rest[-3:]
    i, j = pl.program_id(0), pl.program_id(1)

    @pl.when(j == 0)
    def _():
        h = _pre(x_ref[...], gpre_ref[...], sh_ref[...], sc_ref[...])
        h_scr[...] = h.astype(BF16)
        halo_scr[...] = _halo_rows(h, i)
        acc_scr[...] = jnp.zeros_like(acc_scr)

    def project(stage):
        s, (c0, cw) = stage
        h = h_scr[s * SEG:(s + 1) * SEG, :]
        cols = slice(c0, c0 + cw)
        return tuple(jnp.dot(lhs, w[:, cols], preferred_element_type=F32)
                     for w in (wa_ref, wb_ref) for lhs in (h, halo_scr[...]))

    def consume(stage, up):
        s, (c0, cw) = stage
        cols = slice(c0, c0 + cw)
        a = _conv3(up[0], cwa_ref[:, cols], up[1], s) + cba_ref[:, cols]
        b = _conv3(up[2], cwb_ref[:, cols], up[3], s) + cbb_ref[:, cols]
        act = (_silu(a) * b).astype(BF16)
        acc_scr[s * SEG:(s + 1) * SEG, :] += jnp.dot(act, wd_ref[cols, :], preferred_element_type=F32)

    _pipelined([(s, t) for s in range(N_SEG) for t in _sub_tiles(FFN_TN)], project, consume)

    last = j == pl.num_programs(1) - 1
    if split_out:
        @pl.when(jnp.logical_and(last, i < PROMPT_BLKS))
        def _():
            outs[0][...] = x_ref[...] + gt_ref[...] * _rms(acc_scr[...], gpost_ref[...])

        @pl.when(jnp.logical_and(last, i >= PROMPT_BLKS))
        def _():
            outs[1][...] = x_ref[...] + gt_ref[...] * _rms(acc_scr[...], gpost_ref[...])
    else:
        @pl.when(last)
        def _():
            outs[0][...] = x_ref[...] + gt_ref[...] * _rms(acc_scr[...], gpost_ref[...])


def _ffn_layer(x, mod, gains, layer, w_up, conv_w, conv_b, w_down, split_out=False):
    nj = FFN // FFN_TN
    row = pl.BlockSpec((ROW_BLK, D), lambda i, j: (i, 0))
    if split_out:
        out_specs = _split_specs(D)
        out_shape = [jax.ShapeDtypeStruct((TOK_PROMPT, D), F32), jax.ShapeDtypeStruct((N_TOK - TOK_PROMPT, D), F32)]
    else:
        out_specs, out_shape = [row], [jax.ShapeDtypeStruct((N_TOK, D), F32)]
    res = pl.pallas_call(
        functools.partial(_ffn_kernel, split_out=split_out),
        grid=(N_BLKS, nj),
        in_specs=[
            row, _mod_spec(layer, 3), _mod_spec(layer, 4), _mod_spec(layer, 5),
            _gain_spec(layer, 2), _gain_spec(layer, 3),
            pl.BlockSpec((D, FFN_TN), lambda i, j: (0, j)),
            pl.BlockSpec((D, FFN_TN), lambda i, j: (0, nj + j)),
            pl.BlockSpec((3, FFN_TN), lambda i, j: (0, j)),
            pl.BlockSpec((3, FFN_TN), lambda i, j: (0, nj + j)),
            pl.BlockSpec((1, FFN_TN), lambda i, j: (0, j)),
            pl.BlockSpec((1, FFN_TN), lambda i, j: (0, nj + j)),
            pl.BlockSpec((FFN_TN, D), lambda i, j: (j, 0)),
        ],
        out_specs=out_specs,
        out_shape=out_shape,
        scratch_shapes=[pltpu.VMEM((ROW_BLK, D), BF16), pltpu.VMEM((HALO_ROWS, D), BF16),
                        pltpu.VMEM((ROW_BLK, D), F32)],
        compiler_params=pltpu.CompilerParams(
            dimension_semantics=("arbitrary", "arbitrary"), vmem_limit_bytes=VMEM_LIMIT),
        name="conv_ffn",
    )(x, mod, mod, mod, gains, gains, w_up, w_up, conv_w, conv_w, conv_b, conv_b, w_down)
    return tuple(res) if split_out else res[0]


def _out_kernel(yp_ref, ys_ref, w_ref, *rest, split_x):
    i = pl.program_id(0)
    x_refs, (gt_ref, g_ref, o_ref) = rest[:-3], rest[-3:]
    x = _pick(i, *x_refs) if split_x else x_refs[0][...]
    y = jnp.dot(_pick(i, yp_ref, ys_ref), w_ref[...], preferred_element_type=F32)
    o_ref[...] = x + gt_ref[...] * _rms(y, g_ref[...])


def _out_layer(y_prompt, y_latent, w, x, mod, gains, layer):
    split_x = isinstance(x, tuple)
    row = pl.BlockSpec((ROW_BLK, D), lambda i: (i, 0))
    return pl.pallas_call(
        functools.partial(_out_kernel, split_x=split_x),
        grid=(N_BLKS,),
        in_specs=_split_specs(D) + [pl.BlockSpec((D, D), lambda i: (0, 0))]
        + (_split_specs(D) if split_x else [row]) + [_mod_spec(layer, 2), _gain_spec(layer, 1)],
        out_specs=row,
        out_shape=jax.ShapeDtypeStruct((N_TOK, D), F32),
        compiler_params=pltpu.CompilerParams(
            dimension_semantics=("arbitrary",), vmem_limit_bytes=VMEM_LIMIT),
        name="mixer_out",
    )(y_prompt, y_latent, w, *(x if split_x else (x,)), mod, gains)


def _dn_in_kernel(*refs, split_x):
    x_refs, refs = (refs[:2], refs[2:]) if split_x else (refs[:1], refs[1:])
    sh_ref, sc_ref, gpre_ref, w_ref, cw_ref, wab_ref, alog_ref, dtb_ref, p_ref, g_ref, h_scr, halo_scr = refs
    i, j = pl.program_id(0), pl.program_id(1)

    @pl.when(j == 0)
    def _():
        x = _pick(i, *x_refs) if split_x else x_refs[0][...]
        hf = _pre(x, gpre_ref[...], sh_ref[...], sc_ref[...])
        h = hf.astype(BF16)
        h_scr[...] = h
        halo_scr[...] = _halo_rows(hf, i)
        ab = jnp.dot(h, wab_ref[...], preferred_element_type=F32)
        t = ab + dtb_ref[...]
        softplus = jnp.maximum(t, 0.0) + jnp.log(1.0 + jnp.exp(-jnp.abs(t)))
        lane = lax.broadcasted_iota(jnp.int32, ab.shape, 1)
        g_ref[...] = jnp.where(lane < 2 * HEADS_DN, -jnp.exp(alog_ref[...]) * softplus, jax.nn.sigmoid(ab))

    tiles_per_part = D // DN_TN

    def project(s):
        return jnp.dot(h_scr[s * SEG:(s + 1) * SEG, :], w_ref[...], preferred_element_type=F32)

    @pl.when(j < 2 * tiles_per_part)
    def _():
        halo = jnp.dot(halo_scr[...], w_ref[...], preferred_element_type=F32)
        qscale = jnp.where(j < tiles_per_part, DK ** -0.5, 1.0)

        def consume(s, u):
            act = _silu(_conv3(u, cw_ref[...], halo, s))
            for t in range(DN_TN // DK):
                xs = act[:, t * DK:(t + 1) * DK]
                inv = lax.rsqrt(jnp.sum(xs * xs, axis=-1, keepdims=True) + EPS) * qscale
                p_ref[s * SEG:(s + 1) * SEG, t * DK:(t + 1) * DK] = (xs * inv).astype(BF16)

        _segments_pipelined(project, consume)

    @pl.when(jnp.logical_and(j >= 2 * tiles_per_part, j < 3 * tiles_per_part))
    def _():
        halo = jnp.dot(halo_scr[...], w_ref[...], preferred_element_type=F32)

        def consume(s, u):
            p_ref[s * SEG:(s + 1) * SEG, :] = _silu(_conv3(u, cw_ref[...], halo, s)).astype(BF16)

        _segments_pipelined(project, consume)

    @pl.when(j >= 3 * tiles_per_part)
    def _():
        p_ref[...] = jnp.dot(h_scr[...], w_ref[...], preferred_element_type=F32).astype(BF16)


def _dn_in_layer(x, mod, gains, layer, w_main, conv_w, w_ab, a_log, dt_bias):
    split_x = isinstance(x, tuple)
    row = pl.BlockSpec((ROW_BLK, D), lambda i, j: (i, 0))
    vec = pl.BlockSpec((1, LANES), lambda i, j: (0, 0))
    return pl.pallas_call(
        functools.partial(_dn_in_kernel, split_x=split_x),
        grid=(N_BLKS, DN_MAIN // DN_TN),
        in_specs=(_split_specs(D) if split_x else [row]) + [
            _mod_spec(layer, 0), _mod_spec(layer, 1), _gain_spec(layer, 0),
            pl.BlockSpec((D, DN_TN), lambda i, j: (0, j)),
            pl.BlockSpec((3, DN_TN), lambda i, j: (0, j)),
            pl.BlockSpec((D, LANES), lambda i, j: (0, 0)),
            vec, vec,
        ],
        out_specs=[
            pl.BlockSpec((ROW_BLK, DN_TN), lambda i, j: (i, j)),
            pl.BlockSpec((ROW_BLK, LANES), lambda i, j: (i, 0)),
        ],
        out_shape=[jax.ShapeDtypeStruct((N_TOK, DN_MAIN), BF16), jax.ShapeDtypeStruct((N_TOK, LANES), F32)],
        scratch_shapes=[pltpu.VMEM((ROW_BLK, D), BF16), pltpu.VMEM((HALO_ROWS, D), BF16)],
        compiler_params=pltpu.CompilerParams(
            dimension_semantics=("arbitrary", "arbitrary"), vmem_limit_bytes=VMEM_LIMIT),
        name="deltanet_in",
    )(*(x if split_x else (x,)), mod, mod, gains, w_main, conv_w, w_ab, a_log, dt_bias)


def _bdot_nt(a, b):
    return lax.dot_general(a.astype(BF16), b.astype(BF16), (((1,), (1,)), ((), ())), preferred_element_type=F32)


def _bdot_tn(a, b):
    return lax.dot_general(a.astype(BF16), b.astype(BF16), (((0,), (0,)), ((), ())), preferred_element_type=F32)


def _bmm(a, b):
    return jnp.einsum('gmk,gkn->gmn', a.astype(BF16), b.astype(BF16), preferred_element_type=F32)


def _bmm_nt(a, b):
    return jnp.einsum('gmk,gnk->gmn', a.astype(BF16), b.astype(BF16), preferred_element_type=F32)


SOLVE_BLK = 16


def _invert_diagonal_blocks(tri, reverse):
    G, C, nb = tri.shape[0], CHUNK, CHUNK // SOLVE_BLK
    r = lax.broadcasted_iota(jnp.int32, (C, C), 0)
    c = lax.broadcasted_iota(jnp.int32, (C, C), 1)
    diag = jnp.where(r // SOLVE_BLK == c // SOLVE_BLK, tri, jnp.zeros_like(tri))
    rr = lax.broadcasted_iota(jnp.int32, (C, SOLVE_BLK * LANES), 0)
    cc = lax.broadcasted_iota(jnp.int32, (C, SOLVE_BLK * LANES), 1)
    pick = jnp.where(rr % SOLVE_BLK == cc // LANES, 1.0, 0.0).astype(BF16)
    cols = jnp.einsum('gik,kn->gin', diag, pick, preferred_element_type=F32)
    inv_d = jnp.broadcast_to(jnp.where(r == c, 1.0, 0.0), (G, C, C))
    steps = range(SOLVE_BLK - 1, 0, -1) if reverse else range(SOLVE_BLK - 1)
    for s in steps:
        pivot_rows = jnp.concatenate(
            [jnp.broadcast_to(inv_d[:, b * SOLVE_BLK + s:b * SOLVE_BLK + s + 1, :], (G, SOLVE_BLK, C))
             for b in range(nb)], axis=1)
        inv_d = inv_d - cols[:, :, s * LANES:s * LANES + C] * pivot_rows
    return inv_d.astype(BF16)


def _block_substitution(systems):
    C, nb = CHUNK, CHUNK // SOLVE_BLK
    r = lax.broadcasted_iota(jnp.int32, (C, C), 0)
    c = lax.broadcasted_iota(jnp.int32, (C, C), 1)
    offs = [jnp.where(r // SOLVE_BLK != c // SOLVE_BLK, tri, jnp.zeros_like(tri)) for tri, _, _, _ in systems]

    def placed(block, b):
        G, _, width = block.shape
        parts = []
        if b > 0:
            parts.append(jnp.zeros((G, b * SOLVE_BLK, width), F32))
        parts.append(block)
        if b < nb - 1:
            parts.append(jnp.zeros((G, (nb - 1 - b) * SOLVE_BLK, width), F32))
        return jnp.concatenate(parts, axis=1)

    xs = [None] * len(systems)
    for step in range(nb):
        ys = []
        for n, (_, _, rhs, reverse) in enumerate(systems):
            b = nb - 1 - step if reverse else step
            rows = slice(b * SOLVE_BLK, (b + 1) * SOLVE_BLK)
            ys.append(rhs[:, rows] if xs[n] is None else rhs[:, rows] - _bmm(offs[n][:, rows], xs[n]))
        for n, (_, inv_d, _, reverse) in enumerate(systems):
            b = nb - 1 - step if reverse else step
            rows = slice(b * SOLVE_BLK, (b + 1) * SOLVE_BLK)
            xb = placed(_bmm(inv_d[:, rows], placed(ys[n], b)), b)
            xs[n] = xb if xs[n] is None else xs[n] + xb
    return xs


def _dn_chunk_terms(q, k, v, g_raw, beta, reverse):
    G, C = q.shape[0], CHUNK
    r = lax.broadcasted_iota(jnp.int32, (C, C), 0)
    c = lax.broadcasted_iota(jnp.int32, (C, C), 1)
    incl = (r <= c) if reverse else (r >= c)
    strict = (r < c) if reverse else (r > c)
    g_mat = jnp.broadcast_to(g_raw, (G, C, C))
    scanned = (r >= c) if reverse else (r <= c)
    cum_row = jnp.sum(jnp.where(scanned, g_mat, 0.0), axis=1, keepdims=True)
    cum_col = jnp.sum(jnp.where(r == c, jnp.broadcast_to(cum_row, (G, C, C)), 0.0), axis=2, keepdims=True)
    decay = jnp.exp(jnp.where(incl, cum_col - cum_row, -jnp.inf))
    kb = k * beta
    kk = _bmm_nt(kb, k) * decay
    qk = _bmm_nt(q, k) * decay
    tri = jnp.where(strict, kk, 0.0).astype(BF16)
    e_col = jnp.exp(cum_col)
    rhs = jnp.concatenate([v * beta, kb * e_col], axis=2)
    total = cum_col[:, 0:1] if reverse else cum_col[:, C - 1:C]
    k_end = k * jnp.exp(total - cum_col)
    return tri, rhs, q * e_col, k_end, qk, jnp.exp(total)


PREP_CHUNKS = 4
HEAD_GRP = 4


def _dn_scan_kernel(q_ref, k_ref, v_ref, z_ref, g_ref, s0_ref, on_ref, y_ref, *rest, seq_len, with_final):
    if with_final:
        sfin_ref, rest = rest[0], rest[1:]
    u_scr, w_scr, qe_scr, ke_scr, qk_scr, el_scr, o_scr, s_scr = rest
    n = seq_len // CHUNK
    G, HG = PREP_CHUNKS, HEAD_GRP
    head0 = pl.program_id(1) * HG
    lane = lax.broadcasted_iota(jnp.int32, (G, CHUNK, LANES), 2)

    def prepare(grp, carry):
        rows = pl.ds(pl.multiple_of(grp * (G * CHUNK), G * CHUNK), G * CHUNK)

        def by_head(ref):
            x = ref[rows, :].astype(F32)
            return jnp.concatenate([x[:, hh * DK:(hh + 1) * DK].reshape(G, CHUNK, DK) for hh in range(HG)], axis=0)

        q, k, v = by_head(q_ref), by_head(k_ref), by_head(v_ref)
        gates = g_ref[rows, :].reshape(G, CHUNK, LANES)

        def gate(col):
            return jnp.concatenate(
                [jnp.sum(jnp.where(lane == col + head0 + hh, gates, 0.0), axis=2, keepdims=True) for hh in range(HG)],
                axis=0)

        terms = [_dn_chunk_terms(q, k, v, gate(d * HEADS_DN), gate((2 + d) * HEADS_DN), reverse=(d == 1))
                 for d in range(2)]
        inverses = [_invert_diagonal_blocks(terms[d][0], reverse=(d == 1)) for d in range(2)]
        sols = _block_substitution([(terms[d][0], inverses[d], terms[d][1], d == 1) for d in range(2)])
        for d in range(2):
            _, _, qe, ke, qk, el = terms[d]
            for hh in range(HG):
                part = slice(hh * G, (hh + 1) * G)
                u_scr[d, hh, rows, :] = sols[d][part, :, :DK].reshape(G * CHUNK, DK)
                w_scr[d, hh, rows, :] = sols[d][part, :, DK:].reshape(G * CHUNK, DK).astype(BF16)
                qe_scr[d, hh, rows, :] = qe[part].reshape(G * CHUNK, DK).astype(BF16)
                ke_scr[d, hh, rows, :] = ke[part].reshape(G * CHUNK, DK).astype(BF16)
                qk_scr[d, hh, rows, :] = qk[part].reshape(G * CHUNK, CHUNK).astype(BF16)
                el_scr[d, hh, pl.ds(grp * G, G)] = jnp.broadcast_to(el[part], (G, 8, LANES))
        return carry

    lax.fori_loop(0, n // G, prepare, 0)
    s_scr[...] = s0_ref[...]

    def scan(t, carry):
        chains = [(hh, d) for hh in range(HG) for d in range(2)]
        rows = [pl.ds(pl.multiple_of((t if d == 0 else n - 1 - t) * CHUNK, CHUNK), CHUNK) for _, d in chains]
        states = [s_scr[d, hh] for hh, d in chains]
        from_state = []
        for (hh, d), r, s in zip(chains, rows, states):
            sb = s.astype(BF16)
            from_state.append((jnp.dot(w_scr[d, hh, r, :], sb, preferred_element_type=F32),
                               jnp.dot(qe_scr[d, hh, r, :], sb, preferred_element_type=F32)))
        for (hh, d), r, s, (ws, qs) in zip(chains, rows, states, from_state):
            c = t if d == 0 else n - 1 - t
            vb = (u_scr[d, hh, r, :] - ws).astype(BF16)
            o_scr[d, hh, r, :] = qs + jnp.dot(qk_scr[d, hh, r, :], vb, preferred_element_type=F32)
            s_scr[d, hh] = s * el_scr[d, hh, c][0:1, :] + _bdot_tn(ke_scr[d, hh, r, :], vb)
        return carry

    lax.fori_loop(0, n, scan, 0)
    if with_final:
        sfin_ref[...] = s_scr[...]
    for hh in range(HG):
        cols = slice(hh * DK, (hh + 1) * DK)
        o = o_scr[0, hh] + o_scr[1, hh]
        y_ref[:, cols] = (_rms(o, on_ref[...]) * _silu(z_ref[:, cols].astype(F32))).astype(BF16)


def _dn_scan(p, gates, s0, s0_index, onorm_g, *, n_seq, seq_len, row_offset, with_final):
    blk0 = row_offset // seq_len
    HG = HEAD_GRP
    n_grp = HEADS_DN // HG

    def part(k):
        return pl.BlockSpec((seq_len, HG * DK), lambda b, h: (blk0 + b, k * n_grp + h))

    out_specs = [pl.BlockSpec((seq_len, HG * DK), lambda b, h: (b, h))]
    out_shape = [jax.ShapeDtypeStruct((n_seq * seq_len, D), BF16)]
    if with_final:
        out_specs.append(pl.BlockSpec((None, 2, HG, DK, DK), lambda b, h: (b, 0, h, 0, 0)))
        out_shape.append(jax.ShapeDtypeStruct((n_seq, 2, HEADS_DN, DK, DK), F32))
    n = seq_len // CHUNK
    res = pl.pallas_call(
        functools.partial(_dn_scan_kernel, seq_len=seq_len, with_final=with_final),
        grid=(n_seq, n_grp),
        in_specs=[
            part(0), part(1), part(2), part(3),
            pl.BlockSpec((seq_len, LANES), lambda b, h: (blk0 + b, 0)),
            pl.BlockSpec((None,) * (s0.ndim - 4) + (2, HG, DK, DK), s0_index),
            pl.BlockSpec((1, DK), lambda b, h: (0, 0)),
        ],
        out_specs=out_specs,
        out_shape=out_shape,
        scratch_shapes=[
            pltpu.VMEM((2, HG, seq_len, DK), F32),
            pltpu.VMEM((2, HG, seq_len, DK), BF16),
            pltpu.VMEM((2, HG, seq_len, DK), BF16),
            pltpu.VMEM((2, HG, seq_len, DK), BF16),
            pltpu.VMEM((2, HG, seq_len, CHUNK), BF16),
            pltpu.VMEM((2, HG, n, 8, LANES), F32),
            pltpu.VMEM((2, HG, seq_len, DK), F32),
            pltpu.VMEM((2, HG, DK, DK), F32),
        ],
        compiler_params=pltpu.CompilerParams(
            dimension_semantics=("arbitrary", "arbitrary"), vmem_limit_bytes=VMEM_LIMIT),
        name="deltanet_scan_%d" % seq_len,
    )(p, p, p, p, gates, s0, onorm_g)
    return res


def _rope_tile(x, cos, sin):
    lane = lax.broadcasted_iota(jnp.int32, x.shape, 1)
    width = x.shape[1]
    partner = jnp.where((lane & 31) < 16, pltpu.roll(x, width - 16, 1), pltpu.roll(x, 16, 1))
    return x * cos + partner * sin


def _at_in_kernel(x_ref, sh_ref, sc_ref, gpre_ref, w_ref, cos_ref, sin_ref, q_ref, kv_ref, h_scr):
    i, j = pl.program_id(0), pl.program_id(1)
    q_tiles = AT_Q // AT_TN

    @pl.when(j == 0)
    def _():
        h_scr[...] = _pre(x_ref[...], gpre_ref[...], sh_ref[...], sc_ref[...]).astype(BF16)

    u = jnp.dot(h_scr[...], w_ref[...], preferred_element_type=F32)

    def roped():
        return jnp.concatenate(
            [_rope_tile(u[:, s * LANES:(s + 1) * LANES], cos_ref[...], sin_ref[...]) for s in range(AT_TN // LANES)],
            axis=1)

    is_latent = i >= PROMPT_BLKS

    @pl.when(jnp.logical_and(j < q_tiles, is_latent))
    def _():
        q_ref[...] = (roped() * HD ** -0.5).astype(BF16)

    @pl.when(jnp.logical_and(j < q_tiles, jnp.logical_not(is_latent)))
    def _():
        q_ref[...] = (u * HD ** -0.5).astype(BF16)

    @pl.when(jnp.logical_and(j == q_tiles, is_latent))
    def _():
        kv_ref[...] = roped()

    @pl.when(jnp.logical_or(j > q_tiles, jnp.logical_and(j == q_tiles, jnp.logical_not(is_latent))))
    def _():
        kv_ref[...] = u


def _at_in_layer(x, mod, gains, layer, w_qkv, cos, sin):
    q_tiles = AT_Q // AT_TN
    row = pl.BlockSpec((ROW_BLK, D), lambda i, j: (i, 0))
    tab = pl.BlockSpec((ROW_BLK, LANES), lambda i, j: (0, 0))
    return pl.pallas_call(
        _at_in_kernel,
        grid=(N_BLKS, (AT_Q + 2 * AT_KVW) // AT_TN),
        in_specs=[row, _mod_spec(layer, 0), _mod_spec(layer, 1), _gain_spec(layer, 0),
                  pl.BlockSpec((D, AT_TN), lambda i, j: (0, j)), tab, tab],
        out_specs=[
            pl.BlockSpec((ROW_BLK, AT_TN), lambda i, j: (i, jnp.minimum(j, q_tiles - 1))),
            pl.BlockSpec((ROW_BLK, AT_TN), lambda i, j: (i, jnp.maximum(j - q_tiles, 0))),
        ],
        out_shape=[jax.ShapeDtypeStruct((N_TOK, AT_Q), BF16), jax.ShapeDtypeStruct((N_TOK, 2 * AT_KVW), F32)],
        scratch_shapes=[pltpu.VMEM((ROW_BLK, D), BF16)],
        compiler_params=pltpu.CompilerParams(
            dimension_semantics=("arbitrary", "arbitrary"), vmem_limit_bytes=VMEM_LIMIT),
        name="attention_in",
    )(x, mod, mod, gains, w_qkv, cos, sin)


def _softmax_pv(scores, values, sink):
    m = sink
    for s in scores:
        m = jnp.maximum(m, jnp.max(s, axis=-1, keepdims=True))
    denom = jnp.exp(sink - m)
    acc = None
    for s, v in zip(scores, values):
        p = jnp.exp(s - m)
        denom = denom + jnp.sum(p, axis=-1, keepdims=True)
        pv = jnp.dot(p.astype(BF16), v, preferred_element_type=F32)
        acc = pv if acc is None else acc + pv
    return acc / denom


def _attend_groups(q_ref, scores_of, values_of, sink_ref):
    rows = q_ref.shape[0]

    def stacked_q(kvh):
        return jnp.concatenate([q_ref[:, (kvh * AT_G + g) * HD:(kvh * AT_G + g + 1) * HD] for g in range(AT_G)], axis=0)

    def stacked_sink(kvh):
        return jnp.concatenate([jnp.broadcast_to(sink_ref[0:1, kvh * AT_G + g:kvh * AT_G + g + 1], (rows, 1))
                                for g in range(AT_G)], axis=0)

    outs = []
    pending = scores_of(stacked_q(0), 0)
    for kvh in range(AT_KV):
        upcoming = scores_of(stacked_q(kvh + 1), kvh + 1) if kvh + 1 < AT_KV else None
        o = _softmax_pv(pending, values_of(kvh), stacked_sink(kvh))
        outs.extend(o[g * rows:(g + 1) * rows] for g in range(AT_G))
        pending = upcoming
    return jnp.concatenate(outs, axis=1).astype(BF16)


def _ctx_attn_kernel(q_ref, kv_ref, sink_ref, y_ref):
    keys = [kv_ref[:, kvh * HD:(kvh + 1) * HD].astype(BF16) for kvh in range(AT_KV)]
    vals = [kv_ref[:, AT_KVW + kvh * HD:AT_KVW + (kvh + 1) * HD].astype(BF16) for kvh in range(AT_KV)]
    y_ref[...] = _attend_groups(q_ref, lambda q, kvh: [_bdot_nt(q, keys[kvh])], lambda kvh: [vals[kvh]], sink_ref)


def _ctx_attention(q, kv, sink):
    nq = L_PROMPT // QBLK
    return pl.pallas_call(
        _ctx_attn_kernel,
        grid=(N_PROMPT, nq),
        in_specs=[
            pl.BlockSpec((QBLK, AT_Q), lambda b, t: (b * nq + t, 0)),
            pl.BlockSpec((L_PROMPT, 2 * AT_KVW), lambda b, t: (b, 0)),
            pl.BlockSpec((1, LANES), lambda b, t: (0, 0)),
        ],
        out_specs=pl.BlockSpec((QBLK, AT_Q), lambda b, t: (b * nq + t, 0)),
        out_shape=jax.ShapeDtypeStruct((TOK_PROMPT, AT_Q), BF16),
        compiler_params=pltpu.CompilerParams(dimension_semantics=("arbitrary", "arbitrary")),
        name="context_attention",
    )(q, kv, sink)


def _lat_attn_kernel(q_ref, kv_ref, ck_ref, cv_ref, sink_ref, y_ref):
    t = pl.program_id(1)
    k0 = pl.multiple_of(jnp.clip(t * QBLK - WINDOW, 0, L_LATENT - BAND), QBLK)
    qpos = t * QBLK + lax.broadcasted_iota(jnp.int32, (QBLK, BAND), 0)
    kpos = k0 + lax.broadcasted_iota(jnp.int32, (QBLK, BAND), 1)
    valid = jnp.concatenate([jnp.abs(qpos - kpos) <= WINDOW] * AT_G, axis=0)
    band = pl.ds(k0, BAND)
    k_loc = [kv_ref[band, kvh * HD:(kvh + 1) * HD].astype(BF16) for kvh in range(AT_KV)]
    v_loc = [kv_ref[band, AT_KVW + kvh * HD:AT_KVW + (kvh + 1) * HD].astype(BF16) for kvh in range(AT_KV)]
    k_ctx = [ck_ref[:, kvh * HD:(kvh + 1) * HD].astype(BF16) for kvh in range(AT_KV)]
    v_ctx = [cv_ref[:, kvh * HD:(kvh + 1) * HD].astype(BF16) for kvh in range(AT_KV)]

    def scores_of(q, kvh):
        return [jnp.where(valid, _bdot_nt(q, k_loc[kvh]), -jnp.inf), _bdot_nt(q, k_ctx[kvh])]

    y_ref[...] = _attend_groups(q_ref, scores_of, lambda kvh: [v_loc[kvh], v_ctx[kvh]], sink_ref)


def _lat_attention(q, kv, cache_k, cache_v, sink, attn_layer):
    nq = L_LATENT // QBLK
    q0 = TOK_PROMPT // QBLK
    s0 = TOK_PROMPT // L_LATENT
    cache = pl.BlockSpec((None, None, PAST, AT_KVW), lambda b, t: (b, attn_layer, 0, 0))
    return pl.pallas_call(
        _lat_attn_kernel,
        grid=(N_LATENT, nq),
        in_specs=[
            pl.BlockSpec((QBLK, AT_Q), lambda b, t: (q0 + b * nq + t, 0)),
            pl.BlockSpec((L_LATENT, 2 * AT_KVW), lambda b, t: (s0 + b, 0)),
            cache, cache,
            pl.BlockSpec((1, LANES), lambda b, t: (0, 0)),
        ],
        out_specs=pl.BlockSpec((QBLK, AT_Q), lambda b, t: (b * nq + t, 0)),
        out_shape=jax.ShapeDtypeStruct((N_LATENT * L_LATENT, AT_Q), BF16),
        compiler_params=pltpu.CompilerParams(dimension_semantics=("arbitrary", "arbitrary")),
        name="latent_attention",
    )(q, kv, cache_k, cache_v, sink)


def _rope_tables():
    pos = jnp.arange(L_LATENT)
    half = HD // 2
    inv = jnp.power(ROPE_BASE, -jnp.arange(0, half, 2, dtype=F32) / half)
    ang_row = (pos // GRID_W).astype(F32)[:, None] * inv
    ang_col = (pos % GRID_W).astype(F32)[:, None] * inv
    ang = jnp.concatenate([ang_row, ang_row, ang_col, ang_col], axis=1)
    sign = jnp.tile(jnp.concatenate([-jnp.ones((16,), F32), jnp.ones((16,), F32)]), 2)
    cos = jnp.tile(jnp.cos(ang), (1, 2))
    sin = jnp.tile(jnp.sin(ang) * sign, (1, 2))
    return cos, sin


def _pad_lanes(v):
    v = v.reshape(1, -1).astype(F32)
    return jnp.pad(v, ((0, 0), (0, LANES - v.shape[1])))


def kernel(x_prompt, x_sample, state_delta, cache_k, cache_v, c, c_ctx, w_ada, b_ada, norm_g, dn_w_in, dn_conv_w, dn_a_log, dn_dt_bias, dn_onorm_g, dn_w_out, at_w_qkv, at_sink, at_w_o, ffn_w_up, ffn_conv_w, ffn_conv_b, ffn_w_down):
    x = (x_prompt.reshape(TOK_PROMPT, D), x_sample.reshape(N_TOK - TOK_PROMPT, D))
    cvecs = jnp.concatenate([c_ctx[None, :], c, jnp.zeros((MOD_ROWS - 1 - N_LATENT, D), F32)], axis=0)
    mod = _modulation(cvecs, w_ada, b_ada).reshape(DEPTH, MOD_ROWS, 1, 6 * D)
    gains = norm_g.reshape(DEPTH, 4, 1, D)
    cos, sin = _rope_tables()
    ck = cache_k.reshape(N_LATENT, -1, PAST, AT_KVW)
    cv = cache_v.reshape(N_LATENT, -1, PAST, AT_KVW)
    zero_state = jnp.zeros((2, HEADS_DN, DK, DK), F32)

    states, new_k, new_v = [], [], []
    for layer in range(DEPTH):
        j = layer // 2
        if layer % 2 == 0:
            w_in = dn_w_in[j]
            w_main = w_in[:, :DN_MAIN].astype(BF16)
            w_ab = jnp.pad(w_in[:, DN_MAIN:], ((0, 0), (0, LANES - 4 * HEADS_DN))).astype(BF16)
            conv_w = jnp.pad(dn_conv_w[j], ((0, 0), (0, D)))
            p, gates = _dn_in_layer(x, mod, gains, layer, w_main, conv_w, w_ab,
                                    _pad_lanes(dn_a_log[j]), _pad_lanes(dn_dt_bias[j]))
            onorm = dn_onorm_g[j].reshape(1, DK)
            y_p, s_fin = _dn_scan(p, gates, zero_state, lambda b, h: (0, h, 0, 0), onorm,
                                  n_seq=N_PROMPT, seq_len=L_PROMPT, row_offset=0, with_final=True)
            (y_s,) = _dn_scan(p, gates, state_delta, lambda b, h, j=j: (b, j, 0, h, 0, 0), onorm,
                              n_seq=N_LATENT, seq_len=L_LATENT, row_offset=TOK_PROMPT, with_final=False)
            states.append(s_fin)
            w_o = dn_w_out[j]
        else:
            q, kv = _at_in_layer(x, mod, gains, layer, at_w_qkv[j].astype(BF16), cos, sin)
            sink = _pad_lanes(at_sink[j])
            y_p = _ctx_attention(q, kv, sink)
            y_s = _lat_attention(q, kv, ck, cv, sink, j)
            new_k.append(kv[:TOK_PROMPT, :AT_KVW].reshape(N_PROMPT, L_PROMPT, AT_KV, HD))
            new_v.append(kv[:TOK_PROMPT, AT_KVW:].reshape(N_PROMPT, L_PROMPT, AT_KV, HD))
            w_o = at_w_o[j]
        x = _out_layer(y_p, y_s, w_o.astype(BF16), x, mod, gains, layer)
        x = _ffn_layer(x, mod, gains, layer, ffn_w_up[layer].astype(BF16), ffn_conv_w[layer],
                       ffn_conv_b[layer].reshape(1, 2 * FFN), ffn_w_down[layer].astype(BF16),
                       split_out=(layer == DEPTH - 1))

    y_prompt = x[0].reshape(N_PROMPT, L_PROMPT, D)
    y_sample = x[1].reshape(N_LATENT, L_LATENT, D)
    return (y_prompt, y_sample, jnp.stack(states, axis=1), jnp.stack(new_k, axis=1), jnp.stack(new_v, axis=1))
```

```python
import functools

import jax
import jax.numpy as jnp
from jax import lax
from jax.experimental import pallas as pl
from jax.experimental.pallas import tpu as pltpu

F32 = jnp.float32
BF16 = jnp.bfloat16

D = 1024
N_PROMPT, L_PROMPT = 16, 256
N_LATENT, L_LATENT = 8, 1024
PAST = 512
GRID_W = 64
DEPTH = 4
HEADS_DN, DK = 8, 128
CHUNK = 64
DN_MAIN = 4 * D
AT_HEADS, AT_KV, AT_G, HD = 16, 4, 4, 64
AT_Q = AT_HEADS * HD
AT_KVW = AT_KV * HD
WINDOW = 128
QBLK = 128
BAND = QBLK + 2 * WINDOW
FFN = 2816
EPS = 1e-6
ROPE_BASE = 10000.0
LOG2E = 1.4426950408889634
Q_SCALE = HD ** -0.5 * LOG2E

ROW_BLK = 1024
TOK_PROMPT = N_PROMPT * L_PROMPT
N_TOK = TOK_PROMPT + N_LATENT * L_LATENT
PROMPT_BLKS = TOK_PROMPT // ROW_BLK
N_BLKS = N_TOK // ROW_BLK
MOD_ROWS = 16
LANES = 128
FFN_TN = 1408
MXU_N = 256
DN_TN = 1024
VMEM_LIMIT = 56 * 1024 * 1024


def _mod_row(i):
    return jnp.where(i < PROMPT_BLKS, 0, i - (PROMPT_BLKS - 1))


def _mod_spec(layer, chunk):
    return pl.BlockSpec((None, None, 1, D), lambda i, *_: (layer, _mod_row(i), 0, chunk))


def _gain_spec(layer, k):
    return pl.BlockSpec((None, None, 1, D), lambda i, *_: (layer, k, 0, 0))


def _split_specs(width):
    return [pl.BlockSpec((ROW_BLK, width), lambda i, *_: (jnp.minimum(i, PROMPT_BLKS - 1), 0)),
            pl.BlockSpec((ROW_BLK, width), lambda i, *_: (jnp.maximum(i - PROMPT_BLKS, 0), 0))]


def _pick(i, prompt_ref, latent_ref):
    return jnp.where(i < PROMPT_BLKS, prompt_ref[...], latent_ref[...])


def _rms(x, g):
    return x * lax.rsqrt(jnp.mean(x * x, axis=-1, keepdims=True) + EPS) * g


def _pre(x, g, shift, scale):
    return _rms(x, g) * (1.0 + scale) + shift


def _silu(x):
    return x * jax.nn.sigmoid(x)


SEG = L_PROMPT
N_SEG = ROW_BLK // SEG
HALO_ROWS = 16


def _halo_rows(h, i):
    inside = jnp.where(i < PROMPT_BLKS, 0.0, 1.0)
    picks = []
    for s in range(1, N_SEG):
        picks += [h[s * SEG - 1:s * SEG], h[s * SEG:s * SEG + 1]]
    picks.append(jnp.zeros((HALO_ROWS - len(picks), h.shape[1]), F32))
    return (jnp.concatenate(picks, axis=0) * inside).astype(BF16)


def _conv3(u, w, halo, s):
    sub = lax.broadcasted_iota(jnp.int32, (8, 1), 0)
    zero_row = jnp.zeros((1, u.shape[1]), F32)
    before = halo[2 * s - 2:2 * s - 1] if s > 0 else zero_row
    after = halo[2 * s + 1:2 * s + 2] if s < N_SEG - 1 else zero_row
    down = pltpu.roll(u, 1, 0)
    up = pltpu.roll(u, SEG - 1, 0)
    prev = jnp.concatenate([jnp.where(sub == 0, before, down[:8]), down[8:]], axis=0)
    nxt = jnp.concatenate([up[:SEG - 8], jnp.where(sub == 7, after, up[SEG - 8:])], axis=0)
    return prev * w[0:1] + u * w[1:2] + nxt * w[2:3]


def _pipelined(stages, project, consume, ahead=1):
    pending = [project(st) for st in stages[:ahead]]
    for n, stage in enumerate(stages):
        if n + ahead < len(stages):
            pending.append(project(stages[n + ahead]))
        consume(stage, pending.pop(0))


def _sub_tiles(width):
    tiles, off = [], 0
    while off < width:
        size = min(MXU_N, width - off)
        tiles.append((off, size))
        off += size
    return tiles


def _mod_kernel(c_ref, w_ref, b_ref, o_ref):
    s = _silu(c_ref[...]).astype(BF16)
    o_ref[...] = jnp.dot(s, w_ref[...].astype(BF16), preferred_element_type=F32) + b_ref[...]


def _modulation(cvecs, w_ada, b_ada):
    tn = 1536
    return pl.pallas_call(
        _mod_kernel,
        grid=(DEPTH, 6 * D // tn),
        in_specs=[
            pl.BlockSpec((MOD_ROWS, D), lambda l, j: (0, 0)),
            pl.BlockSpec((None, D, tn), lambda l, j: (l, 0, j)),
            pl.BlockSpec((None, 1, tn), lambda l, j: (l, 0, j)),
        ],
        out_specs=pl.BlockSpec((None, MOD_ROWS, tn), lambda l, j: (l, 0, j)),
        out_shape=jax.ShapeDtypeStruct((DEPTH, MOD_ROWS, 6 * D), F32),
        compiler_params=pltpu.CompilerParams(
            dimension_semantics=("arbitrary", "arbitrary"), vmem_limit_bytes=VMEM_LIMIT),
        name="modulation",
    )(cvecs, w_ada, b_ada.reshape(DEPTH, 1, 6 * D))


def _ffn_kernel(x_ref, sh_ref, sc_ref, gt_ref, gpre_ref, gpost_ref, wa_ref, wb_ref, cwa_ref, cwb_ref,
                cba_ref, cbb_ref, wd_ref, *rest, split_out):
    outs, (h_scr, halo_scr, acc_scr) = rest[:-3], rest[-3:]
    i, j = pl.program_id(0), pl.program_id(1)

    @pl.when(j == 0)
    def _():
        h = _pre(x_ref[...], gpre_ref[...], sh_ref[...], sc_ref[...])
        h_scr[...] = h.astype(BF16)
        halo_scr[...] = _halo_rows(h, i)
        acc_scr[...] = jnp.zeros_like(acc_scr)

    def project(stage):
        s, (c0, cw) = stage
        h = h_scr[s * SEG:(s + 1) * SEG, :]
        cols = slice(c0, c0 + cw)
        return tuple(jnp.dot(lhs, w[:, cols], preferred_element_type=F32)
                     for w in (wa_ref, wb_ref) for lhs in (h, halo_scr[...]))

    def consume(stage, up):
        s, (c0, cw) = stage
        cols = slice(c0, c0 + cw)
        a = _conv3(up[0], cwa_ref[:, cols], up[1], s) + cba_ref[:, cols]
        b = _conv3(up[2], cwb_ref[:, cols], up[3], s) + cbb_ref[:, cols]
        act = (_silu(a) * b).astype(BF16)
        acc_scr[s * SEG:(s + 1) * SEG, :] += jnp.dot(act, wd_ref[cols, :], preferred_element_type=F32)

    _pipelined([(s, t) for s in range(N_SEG) for t in _sub_tiles(FFN_TN)], project, consume)

    last = j == pl.num_programs(1) - 1
    if split_out:
        @pl.when(jnp.logical_and(last, i < PROMPT_BLKS))
        def _():
            outs[0][...] = x_ref[...] + gt_ref[...] * _rms(acc_scr[...], gpost_ref[...])

        @pl.when(jnp.logical_and(last, i >= PROMPT_BLKS))
        def _():
            outs[1][...] = x_ref[...] + gt_ref[...] * _rms(acc_scr[...], gpost_ref[...])
    else:
        @pl.when(last)
        def _():
            outs[0][...] = x_ref[...] + gt_ref[...] * _rms(acc_scr[...], gpost_ref[...])


def _ffn_layer(x, mod, gains, layer, w_up, conv_w, conv_b, w_down, split_out=False):
    nj = FFN // FFN_TN
    row = pl.BlockSpec((ROW_BLK, D), lambda i, j: (i, 0))
    if split_out:
        out_specs = _split_specs(D)
        out_shape = [jax.ShapeDtypeStruct((TOK_PROMPT, D), F32), jax.ShapeDtypeStruct((N_TOK - TOK_PROMPT, D), F32)]
    else:
        out_specs, out_shape = [row], [jax.ShapeDtypeStruct((N_TOK, D), F32)]
    res = pl.pallas_call(
        functools.partial(_ffn_kernel, split_out=split_out),
        grid=(N_BLKS, nj),
        in_specs=[
            row, _mod_spec(layer, 3), _mod_spec(layer, 4), _mod_spec(layer, 5),
            _gain_spec(layer, 2), _gain_spec(layer, 3),
            pl.BlockSpec((D, FFN_TN), lambda i, j: (0, j)),
            pl.BlockSpec((D, FFN_TN), lambda i, j: (0, nj + j)),
            pl.BlockSpec((3, FFN_TN), lambda i, j: (0, j)),
            pl.BlockSpec((3, FFN_TN), lambda i, j: (0, nj + j)),
            pl.BlockSpec((1, FFN_TN), lambda i, j: (0, j)),
            pl.BlockSpec((1, FFN_TN), lambda i, j: (0, nj + j)),
            pl.BlockSpec((FFN_TN, D), lambda i, j: (j, 0)),
        ],
        out_specs=out_specs,
        out_shape=out_shape,
        scratch_shapes=[pltpu.VMEM((ROW_BLK, D), BF16), pltpu.VMEM((HALO_ROWS, D), BF16),
                        pltpu.VMEM((ROW_BLK, D), F32)],
        compiler_params=pltpu.CompilerParams(
            dimension_semantics=("arbitrary", "arbitrary"), vmem_limit_bytes=VMEM_LIMIT),
        name="conv_ffn",
    )(x, mod, mod, mod, gains, gains, w_up, w_up, conv_w, conv_w, conv_b, conv_b, w_down)
    return tuple(res) if split_out else res[0]


def _out_kernel(yp_ref, ys_ref, w_ref, *rest, split_x):
    i = pl.program_id(0)
    x_refs, (gt_ref, g_ref, o_ref) = rest[:-3], rest[-3:]
    x = _pick(i, *x_refs) if split_x else x_refs[0][...]
    y = jnp.dot(_pick(i, yp_ref, ys_ref), w_ref[...], preferred_element_type=F32)
    o_ref[...] = x + gt_ref[...] * _rms(y, g_ref[...])


def _out_layer(y_prompt, y_latent, w, x, mod, gains, layer):
    split_x = isinstance(x, tuple)
    row = pl.BlockSpec((ROW_BLK, D), lambda i: (i, 0))
    return pl.pallas_call(
        functools.partial(_out_kernel, split_x=split_x),
        grid=(N_BLKS,),
        in_specs=_split_specs(D) + [pl.BlockSpec((D, D), lambda i: (0, 0))]
        + (_split_specs(D) if split_x else [row]) + [_mod_spec(layer, 2), _gain_spec(layer, 1)],
        out_specs=row,
        out_shape=jax.ShapeDtypeStruct((N_TOK, D), F32),
        compiler_params=pltpu.CompilerParams(
            dimension_semantics=("arbitrary",), vmem_limit_bytes=VMEM_LIMIT),
        name="mixer_out",
    )(y_prompt, y_latent, w, *(x if split_x else (x,)), mod, gains)


def _dn_in_kernel(*refs, split_x):
    x_refs, refs = (refs[:2], refs[2:]) if split_x else (refs[:1], refs[1:])
    sh_ref, sc_ref, gpre_ref, w_ref, cw_ref, wab_ref, alog_ref, dtb_ref, p_ref, g_ref, h_scr, halo_scr = refs
    i, j = pl.program_id(0), pl.program_id(1)

    @pl.when(j == 0)
    def _():
        x = _pick(i, *x_refs) if split_x else x_refs[0][...]
        hf = _pre(x, gpre_ref[...], sh_ref[...], sc_ref[...])
        h = hf.astype(BF16)
        h_scr[...] = h
        halo_scr[...] = _halo_rows(hf, i)
        ab = jnp.dot(h, wab_ref[...], preferred_element_type=F32)
        t = ab + dtb_ref[...]
        softplus = jnp.maximum(t, 0.0) + jnp.log(1.0 + jnp.exp(-jnp.abs(t)))
        lane = lax.broadcasted_iota(jnp.int32, ab.shape, 1)
        g_ref[...] = jnp.where(lane < 2 * HEADS_DN, -jnp.exp(alog_ref[...]) * softplus, jax.nn.sigmoid(ab))

    stages = [(s, t) for s in range(N_SEG) for t in _sub_tiles(DN_TN)]

    def project(stage, with_halo=True):
        s, (c0, cw) = stage
        w = w_ref[:, c0:c0 + cw]
        u = jnp.dot(h_scr[s * SEG:(s + 1) * SEG, :], w, preferred_element_type=F32)
        return (u, jnp.dot(halo_scr[...], w, preferred_element_type=F32)) if with_halo else u

    def activated(stage, up):
        s, (c0, cw) = stage
        return _silu(_conv3(up[0], cw_ref[:, c0:c0 + cw], up[1], s))

    @pl.when(j < 2)
    def _():
        qscale = jnp.where(j == 0, DK ** -0.5, 1.0)

        def consume(stage, up):
            s, (c0, cw) = stage
            act = activated(stage, up)
            for t in range(cw // DK):
                xs = act[:, t * DK:(t + 1) * DK]
                inv = lax.rsqrt(jnp.sum(xs * xs, axis=-1, keepdims=True) + EPS) * qscale
                p_ref[s * SEG:(s + 1) * SEG, c0 + t * DK:c0 + (t + 1) * DK] = (xs * inv).astype(BF16)

        _pipelined(stages, project, consume)

    @pl.when(j == 2)
    def _():
        def consume(stage, up):
            s, (c0, cw) = stage
            p_ref[s * SEG:(s + 1) * SEG, c0:c0 + cw] = activated(stage, up).astype(BF16)

        _pipelined(stages, project, consume)

    @pl.when(j == 3)
    def _():
        def consume(stage, u):
            s, (c0, cw) = stage
            p_ref[s * SEG:(s + 1) * SEG, c0:c0 + cw] = u.astype(BF16)

        _pipelined(stages, functools.partial(project, with_halo=False), consume)


def _dn_in_layer(x, mod, gains, layer, w_main, conv_w, w_ab, a_log, dt_bias):
    split_x = isinstance(x, tuple)
    row = pl.BlockSpec((ROW_BLK, D), lambda i, j: (i, 0))
    vec = pl.BlockSpec((1, LANES), lambda i, j: (0, 0))
    return pl.pallas_call(
        functools.partial(_dn_in_kernel, split_x=split_x),
        grid=(N_BLKS, DN_MAIN // DN_TN),
        in_specs=(_split_specs(D) if split_x else [row]) + [
            _mod_spec(layer, 0), _mod_spec(layer, 1), _gain_spec(layer, 0),
            pl.BlockSpec((D, DN_TN), lambda i, j: (0, j)),
            pl.BlockSpec((3, DN_TN), lambda i, j: (0, j)),
            pl.BlockSpec((D, LANES), lambda i, j: (0, 0)),
            vec, vec,
        ],
        out_specs=[
            pl.BlockSpec((ROW_BLK, DN_TN), lambda i, j: (i, j)),
            pl.BlockSpec((ROW_BLK, LANES), lambda i, j: (i, 0)),
        ],
        out_shape=[jax.ShapeDtypeStruct((N_TOK, DN_MAIN), BF16), jax.ShapeDtypeStruct((N_TOK, LANES), F32)],
        scratch_shapes=[pltpu.VMEM((ROW_BLK, D), BF16), pltpu.VMEM((HALO_ROWS, D), BF16)],
        compiler_params=pltpu.CompilerParams(
            dimension_semantics=("arbitrary", "arbitrary"), vmem_limit_bytes=VMEM_LIMIT),
        name="deltanet_in",
    )(*(x if split_x else (x,)), mod, mod, gains, w_main, conv_w, w_ab, a_log, dt_bias)


def _bdot_nt(a, b):
    return lax.dot_general(a.astype(BF16), b.astype(BF16), (((1,), (1,)), ((), ())), preferred_element_type=F32)


def _bdot_tn(a, b):
    return lax.dot_general(a.astype(BF16), b.astype(BF16), (((0,), (0,)), ((), ())), preferred_element_type=F32)


def _bmm(a, b):
    return jnp.einsum('gmk,gkn->gmn', a.astype(BF16), b.astype(BF16), preferred_element_type=F32)


def _bmm_nt(a, b):
    return jnp.einsum('gmk,gnk->gmn', a.astype(BF16), b.astype(BF16), preferred_element_type=F32)


SOLVE_BLK = 16


def _invert_diagonal_blocks(tri, reverse):
    G, C, nb = tri.shape[0], CHUNK, CHUNK // SOLVE_BLK
    r = lax.broadcasted_iota(jnp.int32, (C, C), 0)
    c = lax.broadcasted_iota(jnp.int32, (C, C), 1)
    diag = jnp.where(r // SOLVE_BLK == c // SOLVE_BLK, tri, jnp.zeros_like(tri))
    rr = lax.broadcasted_iota(jnp.int32, (C, SOLVE_BLK * LANES), 0)
    cc = lax.broadcasted_iota(jnp.int32, (C, SOLVE_BLK * LANES), 1)
    pick = jnp.where(rr % SOLVE_BLK == cc // LANES, 1.0, 0.0).astype(BF16)
    cols = jnp.einsum('gik,kn->gin', diag, pick, preferred_element_type=F32)
    inv_d = jnp.broadcast_to(jnp.where(r == c, 1.0, 0.0), (G, C, C))
    steps = range(SOLVE_BLK - 1, 0, -1) if reverse else range(SOLVE_BLK - 1)
    for s in steps:
        pivot_rows = jnp.concatenate(
            [jnp.broadcast_to(inv_d[:, b * SOLVE_BLK + s:b * SOLVE_BLK + s + 1, :], (G, SOLVE_BLK, C))
             for b in range(nb)], axis=1)
        inv_d = inv_d - cols[:, :, s * LANES:s * LANES + C] * pivot_rows
    return inv_d.astype(BF16)


def _block_substitution(systems):
    C, nb = CHUNK, CHUNK // SOLVE_BLK
    r = lax.broadcasted_iota(jnp.int32, (C, C), 0)
    c = lax.broadcasted_iota(jnp.int32, (C, C), 1)
    offs = [jnp.where(r // SOLVE_BLK != c // SOLVE_BLK, tri, jnp.zeros_like(tri)) for tri, _, _, _ in systems]

    def placed(block, b):
        G, _, width = block.shape
        parts = []
        if b > 0:
            parts.append(jnp.zeros((G, b * SOLVE_BLK, width), F32))
        parts.append(block)
        if b < nb - 1:
            parts.append(jnp.zeros((G, (nb - 1 - b) * SOLVE_BLK, width), F32))
        return jnp.concatenate(parts, axis=1)

    xs = [None] * len(systems)
    for step in range(nb):
        ys = []
        for n, (_, _, rhs, reverse) in enumerate(systems):
            b = nb - 1 - step if reverse else step
            rows = slice(b * SOLVE_BLK, (b + 1) * SOLVE_BLK)
            ys.append(rhs[:, rows] if xs[n] is None else rhs[:, rows] - _bmm(offs[n][:, rows], xs[n]))
        for n, (_, inv_d, _, reverse) in enumerate(systems):
            b = nb - 1 - step if reverse else step
            rows = slice(b * SOLVE_BLK, (b + 1) * SOLVE_BLK)
            xb = placed(_bmm(inv_d[:, rows], placed(ys[n], b)), b)
            xs[n] = xb if xs[n] is None else xs[n] + xb
    return xs


def _dn_chunk_terms(q, k, v, g_raw, beta, reverse):
    G, C = q.shape[0], CHUNK
    r = lax.broadcasted_iota(jnp.int32, (C, C), 0)
    c = lax.broadcasted_iota(jnp.int32, (C, C), 1)
    incl = (r <= c) if reverse else (r >= c)
    strict = (r < c) if reverse else (r > c)
    g_mat = jnp.broadcast_to(g_raw, (G, C, C))
    scanned = (r >= c) if reverse else (r <= c)
    cum_row = jnp.sum(jnp.where(scanned, g_mat, 0.0), axis=1, keepdims=True)
    cum_col = jnp.sum(jnp.where(r == c, jnp.broadcast_to(cum_row, (G, C, C)), 0.0), axis=2, keepdims=True)
    decay = jnp.exp(jnp.where(incl, cum_col - cum_row, -jnp.inf))
    kb = k * beta
    kk = _bmm_nt(kb, k) * decay
    qk = _bmm_nt(q, k) * decay
    tri = jnp.where(strict, kk, 0.0).astype(BF16)
    e_col = jnp.exp(cum_col)
    rhs = jnp.concatenate([v * beta, kb * e_col], axis=2)
    total = cum_col[:, 0:1] if reverse else cum_col[:, C - 1:C]
    k_end = k * jnp.exp(total - cum_col)
    return tri, rhs, q * e_col, k_end, qk, jnp.exp(total)


PREP_CHUNKS = 4
HEAD_GRP = 4


def _dn_scan_kernel(q_ref, k_ref, v_ref, z_ref, g_ref, s0_ref, on_ref, y_ref, *rest, seq_len, with_final):
    if with_final:
        sfin_ref, rest = rest[0], rest[1:]
    u_scr, w_scr, qe_scr, ke_scr, qk_scr, el_scr, o_scr, s_scr = rest
    n = seq_len // CHUNK
    G, HG = PREP_CHUNKS, HEAD_GRP
    head0 = pl.program_id(1) * HG
    lane = lax.broadcasted_iota(jnp.int32, (G, CHUNK, LANES), 2)

    def prepare(grp, carry):
        rows = pl.ds(pl.multiple_of(grp * (G * CHUNK), G * CHUNK), G * CHUNK)

        def by_head(ref):
            x = ref[rows, :].astype(F32)
            return jnp.concatenate([x[:, hh * DK:(hh + 1) * DK].reshape(G, CHUNK, DK) for hh in range(HG)], axis=0)

        q, k, v = by_head(q_ref), by_head(k_ref), by_head(v_ref)
        gates = g_ref[rows, :].reshape(G, CHUNK, LANES)

        def gate(col):
            return jnp.concatenate(
                [jnp.sum(jnp.where(lane == col + head0 + hh, gates, 0.0), axis=2, keepdims=True) for hh in range(HG)],
                axis=0)

        terms = [_dn_chunk_terms(q, k, v, gate(d * HEADS_DN), gate((2 + d) * HEADS_DN), reverse=(d == 1))
                 for d in range(2)]
        inverses = [_invert_diagonal_blocks(terms[d][0], reverse=(d == 1)) for d in range(2)]
        sols = _block_substitution([(terms[d][0], inverses[d], terms[d][1], d == 1) for d in range(2)])
        for d in range(2):
            _, _, qe, ke, qk, el = terms[d]
            for hh in range(HG):
                part = slice(hh * G, (hh + 1) * G)
                u_scr[d, hh, rows, :] = sols[d][part, :, :DK].reshape(G * CHUNK, DK)
                w_scr[d, hh, rows, :] = sols[d][part, :, DK:].reshape(G * CHUNK, DK).astype(BF16)
                qe_scr[d, hh, rows, :] = qe[part].reshape(G * CHUNK, DK).astype(BF16)
                ke_scr[d, hh, rows, :] = ke[part].reshape(G * CHUNK, DK).astype(BF16)
                qk_scr[d, hh, rows, :] = qk[part].reshape(G * CHUNK, CHUNK).astype(BF16)
                el_scr[d, hh, pl.ds(grp * G, G)] = jnp.broadcast_to(el[part], (G, 8, LANES))
        return carry

    lax.fori_loop(0, n // G, prepare, 0)
    s_scr[...] = s0_ref[...]

    def scan(t, carry):
        chains = [(hh, d) for hh in range(HG) for d in range(2)]
        rows = [pl.ds(pl.multiple_of((t if d == 0 else n - 1 - t) * CHUNK, CHUNK), CHUNK) for _, d in chains]
        states = [s_scr[d, hh] for hh, d in chains]
        from_state = []
        for (hh, d), r, s in zip(chains, rows, states):
            sb = s.astype(BF16)
            from_state.append((jnp.dot(w_scr[d, hh, r, :], sb, preferred_element_type=F32),
                               jnp.dot(qe_scr[d, hh, r, :], sb, preferred_element_type=F32)))
        for (hh, d), r, s, (ws, qs) in zip(chains, rows, states, from_state):
            c = t if d == 0 else n - 1 - t
            vb = (u_scr[d, hh, r, :] - ws).astype(BF16)
            o_scr[d, hh, r, :] = qs + jnp.dot(qk_scr[d, hh, r, :], vb, preferred_element_type=F32)
            s_scr[d, hh] = s * el_scr[d, hh, c][0:1, :] + _bdot_tn(ke_scr[d, hh, r, :], vb)
        return carry

    lax.fori_loop(0, n, scan, 0)
    if with_final:
        sfin_ref[...] = s_scr[...]
    for hh in range(HG):
        cols = slice(hh * DK, (hh + 1) * DK)
        o = o_scr[0, hh] + o_scr[1, hh]
        y_ref[:, cols] = (_rms(o, on_ref[...]) * _silu(z_ref[:, cols].astype(F32))).astype(BF16)


def _dn_scan(p, gates, s0, s0_index, onorm_g, *, n_seq, seq_len, row_offset, with_final):
    blk0 = row_offset // seq_len
    HG = HEAD_GRP
    n_grp = HEADS_DN // HG

    def part(k):
        return pl.BlockSpec((seq_len, HG * DK), lambda b, h: (blk0 + b, k * n_grp + h))

    out_specs = [pl.BlockSpec((seq_len, HG * DK), lambda b, h: (b, h))]
    out_shape = [jax.ShapeDtypeStruct((n_seq * seq_len, D), BF16)]
    if with_final:
        out_specs.append(pl.BlockSpec((None, 2, HG, DK, DK), lambda b, h: (b, 0, h, 0, 0)))
        out_shape.append(jax.ShapeDtypeStruct((n_seq, 2, HEADS_DN, DK, DK), F32))
    n = seq_len // CHUNK
    res = pl.pallas_call(
        functools.partial(_dn_scan_kernel, seq_len=seq_len, with_final=with_final),
        grid=(n_seq, n_grp),
        in_specs=[
            part(0), part(1), part(2), part(3),
            pl.BlockSpec((seq_len, LANES), lambda b, h: (blk0 + b, 0)),
            pl.BlockSpec((None,) * (s0.ndim - 4) + (2, HG, DK, DK), s0_index),
            pl.BlockSpec((1, DK), lambda b, h: (0, 0)),
        ],
        out_specs=out_specs,
        out_shape=out_shape,
        scratch_shapes=[
            pltpu.VMEM((2, HG, seq_len, DK), F32),
            pltpu.VMEM((2, HG, seq_len, DK), BF16),
            pltpu.VMEM((2, HG, seq_len, DK), BF16),
            pltpu.VMEM((2, HG, seq_len, DK), BF16),
            pltpu.VMEM((2, HG, seq_len, CHUNK), BF16),
            pltpu.VMEM((2, HG, n, 8, LANES), F32),
            pltpu.VMEM((2, HG, seq_len, DK), F32),
            pltpu.VMEM((2, HG, DK, DK), F32),
        ],
        compiler_params=pltpu.CompilerParams(
            dimension_semantics=("arbitrary", "arbitrary"), vmem_limit_bytes=VMEM_LIMIT),
        name="deltanet_scan_%d" % seq_len,
    )(p, p, p, p, gates, s0, onorm_g)
    return res


def _rope_tile(x, cos, sin):
    lane = lax.broadcasted_iota(jnp.int32, x.shape, 1)
    width = x.shape[1]
    partner = jnp.where((lane & 31) < 16, pltpu.roll(x, width - 16, 1), pltpu.roll(x, 16, 1))
    return x * cos + partner * sin


def _at_in_kernel(x_ref, sh_ref, sc_ref, gpre_ref, w_ref, cos_ref, sin_ref, q_ref, kv_ref, h_scr):
    i = pl.program_id(0)
    h_scr[...] = _pre(x_ref[...], gpre_ref[...], sh_ref[...], sc_ref[...]).astype(BF16)
    stages = [(s, t) for s in range(N_SEG) for t in _sub_tiles(AT_Q + 2 * AT_KVW)]

    def project(stage):
        s, (c0, cw) = stage
        return jnp.dot(h_scr[s * SEG:(s + 1) * SEG, :], w_ref[:, c0:c0 + cw], preferred_element_type=F32)

    def consume(stage, u, rope):
        s, (c0, cw) = stage
        rows = slice(s * SEG, (s + 1) * SEG)
        if rope and c0 < AT_Q + AT_KVW:
            u = jnp.concatenate([_rope_tile(u[:, t * LANES:(t + 1) * LANES], cos_ref[rows, :], sin_ref[rows, :])
                                 for t in range(cw // LANES)], axis=1)
        if c0 < AT_Q:
            q_ref[rows, c0:c0 + cw] = (u * Q_SCALE).astype(BF16)
        else:
            kv_ref[rows, c0 - AT_Q:c0 - AT_Q + cw] = u

    @pl.when(i >= PROMPT_BLKS)
    def _():
        _pipelined(stages, project, functools.partial(consume, rope=True))

    @pl.when(i < PROMPT_BLKS)
    def _():
        _pipelined(stages, project, functools.partial(consume, rope=False))


def _at_in_layer(x, mod, gains, layer, w_qkv, cos, sin):
    width = AT_Q + 2 * AT_KVW
    row = pl.BlockSpec((ROW_BLK, D), lambda i: (i, 0))
    tab = pl.BlockSpec((ROW_BLK, LANES), lambda i: (0, 0))
    return pl.pallas_call(
        _at_in_kernel,
        grid=(N_BLKS,),
        in_specs=[row, _mod_spec(layer, 0), _mod_spec(layer, 1), _gain_spec(layer, 0),
                  pl.BlockSpec((D, width), lambda i: (0, 0)), tab, tab],
        out_specs=[pl.BlockSpec((ROW_BLK, AT_Q), lambda i: (i, 0)),
                   pl.BlockSpec((ROW_BLK, 2 * AT_KVW), lambda i: (i, 0))],
        out_shape=[jax.ShapeDtypeStruct((N_TOK, AT_Q), BF16), jax.ShapeDtypeStruct((N_TOK, 2 * AT_KVW), F32)],
        scratch_shapes=[pltpu.VMEM((ROW_BLK, D), BF16)],
        compiler_params=pltpu.CompilerParams(
            dimension_semantics=("arbitrary",), vmem_limit_bytes=VMEM_LIMIT),
        name="attention_in",
    )(x, mod, mod, gains, w_qkv, cos, sin)


def _softmax_pv(scores, values, sink):
    m = sink
    for s in scores:
        m = jnp.maximum(m, jnp.max(s, axis=0, keepdims=True))
    denom = jnp.exp2(sink - m)
    acc = None
    for s, v in zip(scores, values):
        p = jnp.exp2(s - m)
        denom = denom + jnp.sum(p, axis=0, keepdims=True)
        pv = _bdot_tn(v, p)
        acc = pv if acc is None else acc + pv
    return acc / denom


def _attend_groups(q_ref, scores_of, values_of, sink_ref):
    rows = q_ref.shape[0]

    def stacked_q(kvh):
        return jnp.concatenate([q_ref[:, (kvh * AT_G + g) * HD:(kvh * AT_G + g + 1) * HD] for g in range(AT_G)], axis=0)

    def stacked_sink(kvh):
        return jnp.concatenate([jnp.broadcast_to(sink_ref[0:1, kvh * AT_G + g:kvh * AT_G + g + 1], (1, rows))
                                for g in range(AT_G)], axis=1)

    outs = []
    pending = scores_of(stacked_q(0), 0)
    for kvh in range(AT_KV):
        upcoming = scores_of(stacked_q(kvh + 1), kvh + 1) if kvh + 1 < AT_KV else None
        o = _softmax_pv(pending, values_of(kvh), stacked_sink(kvh))
        outs.extend(o[:, g * rows:(g + 1) * rows] for g in range(AT_G))
        pending = upcoming
    return jnp.concatenate(outs, axis=0).T.astype(BF16)


def _ctx_attn_kernel(q_ref, kv_ref, sink_ref, y_ref):
    keys = [kv_ref[:, kvh * HD:(kvh + 1) * HD].astype(BF16) for kvh in range(AT_KV)]
    vals = [kv_ref[:, AT_KVW + kvh * HD:AT_KVW + (kvh + 1) * HD].astype(BF16) for kvh in range(AT_KV)]
    y_ref[...] = _attend_groups(q_ref, lambda q, kvh: [_bdot_nt(keys[kvh], q)], lambda kvh: [vals[kvh]], sink_ref)


def _ctx_attention(q, kv, sink):
    nq = L_PROMPT // QBLK
    return pl.pallas_call(
        _ctx_attn_kernel,
        grid=(N_PROMPT, nq),
        in_specs=[
            pl.BlockSpec((QBLK, AT_Q), lambda b, t: (b * nq + t, 0)),
            pl.BlockSpec((L_PROMPT, 2 * AT_KVW), lambda b, t: (b, 0)),
            pl.BlockSpec((1, LANES), lambda b, t: (0, 0)),
        ],
        out_specs=pl.BlockSpec((QBLK, AT_Q), lambda b, t: (b * nq + t, 0)),
        out_shape=jax.ShapeDtypeStruct((TOK_PROMPT, AT_Q), BF16),
        compiler_params=pltpu.CompilerParams(dimension_semantics=("arbitrary", "arbitrary")),
        name="context_attention",
    )(q, kv, sink)


def _lat_attn_kernel(q_ref, kv_ref, ck_ref, cv_ref, sink_ref, y_ref):
    t = pl.program_id(1)
    k0 = pl.multiple_of(jnp.clip(t * QBLK - WINDOW, 0, L_LATENT - BAND), QBLK)
    kpos = k0 + lax.broadcasted_iota(jnp.int32, (BAND, QBLK), 0)
    qpos = t * QBLK + lax.broadcasted_iota(jnp.int32, (BAND, QBLK), 1)
    valid = jnp.concatenate([jnp.abs(qpos - kpos) <= WINDOW] * AT_G, axis=1)
    band = pl.ds(k0, BAND)
    k_loc = [kv_ref[band, kvh * HD:(kvh + 1) * HD].astype(BF16) for kvh in range(AT_KV)]
    v_loc = [kv_ref[band, AT_KVW + kvh * HD:AT_KVW + (kvh + 1) * HD].astype(BF16) for kvh in range(AT_KV)]
    k_ctx = [ck_ref[:, kvh * HD:(kvh + 1) * HD].astype(BF16) for kvh in range(AT_KV)]
    v_ctx = [cv_ref[:, kvh * HD:(kvh + 1) * HD].astype(BF16) for kvh in range(AT_KV)]

    def scores_of(q, kvh):
        return [jnp.where(valid, _bdot_nt(k_loc[kvh], q), -jnp.inf), _bdot_nt(k_ctx[kvh], q)]

    y_ref[...] = _attend_groups(q_ref, scores_of, lambda kvh: [v_loc[kvh], v_ctx[kvh]], sink_ref)


def _lat_attention(q, kv, cache_k, cache_v, sink, attn_layer):
    nq = L_LATENT // QBLK
    q0 = TOK_PROMPT // QBLK
    s0 = TOK_PROMPT // L_LATENT
    cache = pl.BlockSpec((None, None, PAST, AT_KVW), lambda b, t: (b, attn_layer, 0, 0))
    return pl.pallas_call(
        _lat_attn_kernel,
        grid=(N_LATENT, nq),
        in_specs=[
            pl.BlockSpec((QBLK, AT_Q), lambda b, t: (q0 + b * nq + t, 0)),
            pl.BlockSpec((L_LATENT, 2 * AT_KVW), lambda b, t: (s0 + b, 0)),
            cache, cache,
            pl.BlockSpec((1, LANES), lambda b, t: (0, 0)),
        ],
        out_specs=pl.BlockSpec((QBLK, AT_Q), lambda b, t: (b * nq + t, 0)),
        out_shape=jax.ShapeDtypeStruct((N_LATENT * L_LATENT, AT_Q), BF16),
        compiler_params=pltpu.CompilerParams(dimension_semantics=("arbitrary", "arbitrary")),
        name="latent_attention",
    )(q, kv, cache_k, cache_v, sink)


def _rope_tables():
    pos = jnp.arange(L_LATENT)
    half = HD // 2
    inv = jnp.power(ROPE_BASE, -jnp.arange(0, half, 2, dtype=F32) / half)
    ang_row = (pos // GRID_W).astype(F32)[:, None] * inv
    ang_col = (pos % GRID_W).astype(F32)[:, None] * inv
    ang = jnp.concatenate([ang_row, ang_row, ang_col, ang_col], axis=1)
    sign = jnp.tile(jnp.concatenate([-jnp.ones((16,), F32), jnp.ones((16,), F32)]), 2)
    cos = jnp.tile(jnp.cos(ang), (1, 2))
    sin = jnp.tile(jnp.sin(ang) * sign, (1, 2))
    return cos, sin


def _pad_lanes(v):
    v = v.reshape(1, -1).astype(F32)
    return jnp.pad(v, ((0, 0), (0, LANES - v.shape[1])))


def kernel(x_prompt, x_sample, state_delta, cache_k, cache_v, c, c_ctx, w_ada, b_ada, norm_g, dn_w_in, dn_conv_w, dn_a_log, dn_dt_bias, dn_onorm_g, dn_w_out, at_w_qkv, at_sink, at_w_o, ffn_w_up, ffn_conv_w, ffn_conv_b, ffn_w_down):
    x = (x_prompt.reshape(TOK_PROMPT, D), x_sample.reshape(N_TOK - TOK_PROMPT, D))
    cvecs = jnp.concatenate([c_ctx[None, :], c, jnp.zeros((MOD_ROWS - 1 - N_LATENT, D), F32)], axis=0)
    mod = _modulation(cvecs, w_ada, b_ada).reshape(DEPTH, MOD_ROWS, 1, 6 * D)
    gains = norm_g.reshape(DEPTH, 4, 1, D)
    cos, sin = _rope_tables()
    ck = cache_k.reshape(N_LATENT, -1, PAST, AT_KVW)
    cv = cache_v.reshape(N_LATENT, -1, PAST, AT_KVW)
    zero_state = jnp.zeros((2, HEADS_DN, DK, DK), F32)

    states, new_k, new_v = [], [], []
    for layer in range(DEPTH):
        j = layer // 2
        if layer % 2 == 0:
            w_in = dn_w_in[j]
            w_main = w_in[:, :DN_MAIN].astype(BF16)
            w_ab = jnp.pad(w_in[:, DN_MAIN:], ((0, 0), (0, LANES - 4 * HEADS_DN))).astype(BF16)
            conv_w = jnp.pad(dn_conv_w[j], ((0, 0), (0, D)))
            p, gates = _dn_in_layer(x, mod, gains, layer, w_main, conv_w, w_ab,
                                    _pad_lanes(dn_a_log[j]), _pad_lanes(dn_dt_bias[j]))
            onorm = dn_onorm_g[j].reshape(1, DK)
            y_p, s_fin = _dn_scan(p, gates, zero_state, lambda b, h: (0, h, 0, 0), onorm,
                                  n_seq=N_PROMPT, seq_len=L_PROMPT, row_offset=0, with_final=True)
            (y_s,) = _dn_scan(p, gates, state_delta, lambda b, h, j=j: (b, j, 0, h, 0, 0), onorm,
                              n_seq=N_LATENT, seq_len=L_LATENT, row_offset=TOK_PROMPT, with_final=False)
            states.append(s_fin)
            w_o = dn_w_out[j]
        else:
            q, kv = _at_in_layer(x, mod, gains, layer, at_w_qkv[j].astype(BF16), cos, sin)
            sink = _pad_lanes(at_sink[j]) * LOG2E
            y_p = _ctx_attention(q, kv, sink)
            y_s = _lat_attention(q, kv, ck, cv, sink, j)
            new_k.append(kv[:TOK_PROMPT, :AT_KVW].reshape(N_PROMPT, L_PROMPT, AT_KV, HD))
            new_v.append(kv[:TOK_PROMPT, AT_KVW:].reshape(N_PROMPT, L_PROMPT, AT_KV, HD))
            w_o = at_w_o[j]
        x = _out_layer(y_p, y_s, w_o.astype(BF16), x, mod, gains, layer)
        x = _ffn_layer(x, mod, gains, layer, ffn_w_up[layer].astype(BF16), ffn_conv_w[layer],
                       ffn_conv_b[layer].reshape(1, 2 * FFN), ffn_w_down[layer].astype(BF16),
                       split_out=(layer == DEPTH - 1))

    y_prompt = x[0].reshape(N_PROMPT, L_PROMPT, D)
    y_sample = x[1].reshape(N_LATENT, L_LATENT, D)
    return (y_prompt, y_sample, jnp.stack(states, axis=1), jnp.stack(new_k, axis=1), jnp.stack(new_v, axis=1))
```

```python
import functools

import jax
import jax.numpy as jnp
from jax import lax
from jax.experimental import pallas as pl
from jax.experimental.pallas import tpu as pltpu

F32 = jnp.float32
BF16 = jnp.bfloat16

D = 1024
N_PROMPT, L_PROMPT = 16, 256
N_LATENT, L_LATENT = 8, 1024
PAST = 512
GRID_W = 64
DEPTH = 4
HEADS_DN, DK = 8, 128
CHUNK = 64
DN_MAIN = 4 * D
AT_HEADS, AT_KV, AT_G, HD = 16, 4, 4, 64
AT_Q = AT_HEADS * HD
AT_KVW = AT_KV * HD
WINDOW = 128
QBLK = 128
BAND = QBLK + 2 * WINDOW
FFN = 2816
EPS = 1e-6
ROPE_BASE = 10000.0
LOG2E = 1.4426950408889634
Q_SCALE = HD ** -0.5 * LOG2E

ROW_BLK = 1024
TOK_PROMPT = N_PROMPT * L_PROMPT
N_TOK = TOK_PROMPT + N_LATENT * L_LATENT
PROMPT_BLKS = TOK_PROMPT // ROW_BLK
N_BLKS = N_TOK // ROW_BLK
MOD_ROWS = 16
LANES = 128
FFN_TN = 1408
MXU_N = 256
DN_TN = 1024
VMEM_LIMIT = 56 * 1024 * 1024


def _mod_row(i):
    return jnp.where(i < PROMPT_BLKS, 0, i - (PROMPT_BLKS - 1))


def _mod_spec(layer, chunk):
    return pl.BlockSpec((None, None, 1, D), lambda i, *_: (layer, _mod_row(i), 0, chunk))


def _gain_spec(layer, k):
    return pl.BlockSpec((None, None, 1, D), lambda i, *_: (layer, k, 0, 0))


def _split_specs(width):
    return [pl.BlockSpec((ROW_BLK, width), lambda i, *_: (jnp.minimum(i, PROMPT_BLKS - 1), 0)),
            pl.BlockSpec((ROW_BLK, width), lambda i, *_: (jnp.maximum(i - PROMPT_BLKS, 0), 0))]


def _pick(i, prompt_ref, latent_ref):
    return jnp.where(i < PROMPT_BLKS, prompt_ref[...], latent_ref[...])


def _rms(x, g):
    return x * lax.rsqrt(jnp.mean(x * x, axis=-1, keepdims=True) + EPS) * g


def _pre(x, g, shift, scale):
    return _rms(x, g) * (1.0 + scale) + shift


def _silu(x):
    return x * jax.nn.sigmoid(x)


SEG = L_PROMPT
N_SEG = ROW_BLK // SEG
HALO_ROWS = 16


def _halo_rows(h, i):
    inside = jnp.where(i < PROMPT_BLKS, 0.0, 1.0)
    picks = []
    for s in range(1, N_SEG):
        picks += [h[s * SEG - 1:s * SEG], h[s * SEG:s * SEG + 1]]
    picks.append(jnp.zeros((HALO_ROWS - len(picks), h.shape[1]), F32))
    return (jnp.concatenate(picks, axis=0) * inside).astype(BF16)


def _conv3(u, w, halo, s):
    sub = lax.broadcasted_iota(jnp.int32, (8, 1), 0)
    zero_row = jnp.zeros((1, u.shape[1]), F32)
    before = halo[2 * s - 2:2 * s - 1] if s > 0 else zero_row
    after = halo[2 * s + 1:2 * s + 2] if s < N_SEG - 1 else zero_row
    down = pltpu.roll(u, 1, 0)
    up = pltpu.roll(u, SEG - 1, 0)
    prev = jnp.concatenate([jnp.where(sub == 0, before, down[:8]), down[8:]], axis=0)
    nxt = jnp.concatenate([up[:SEG - 8], jnp.where(sub == 7, after, up[SEG - 8:])], axis=0)
    return prev * w[0:1] + u * w[1:2] + nxt * w[2:3]


def _pipelined(stages, project, consume, ahead=1):
    pending = [project(st) for st in stages[:ahead]]
    for n, stage in enumerate(stages):
        if n + ahead < len(stages):
            pending.append(project(stages[n + ahead]))
        consume(stage, pending.pop(0))


def _sub_tiles(width):
    tiles, off = [], 0
    while off < width:
        size = min(MXU_N, width - off)
        tiles.append((off, size))
        off += size
    return tiles


def _mod_kernel(c_ref, w_ref, b_ref, o_ref):
    s = _silu(c_ref[...]).astype(BF16)
    o_ref[...] = jnp.dot(s, w_ref[...].astype(BF16), preferred_element_type=F32) + b_ref[...]


def _modulation(cvecs, w_ada, b_ada):
    tn = 1536
    return pl.pallas_call(
        _mod_kernel,
        grid=(DEPTH, 6 * D // tn),
        in_specs=[
            pl.BlockSpec((MOD_ROWS, D), lambda l, j: (0, 0)),
            pl.BlockSpec((None, D, tn), lambda l, j: (l, 0, j)),
            pl.BlockSpec((None, 1, tn), lambda l, j: (l, 0, j)),
        ],
        out_specs=pl.BlockSpec((None, MOD_ROWS, tn), lambda l, j: (l, 0, j)),
        out_shape=jax.ShapeDtypeStruct((DEPTH, MOD_ROWS, 6 * D), F32),
        compiler_params=pltpu.CompilerParams(
            dimension_semantics=("arbitrary", "arbitrary"), vmem_limit_bytes=VMEM_LIMIT),
        name="modulation",
    )(cvecs, w_ada, b_ada.reshape(DEPTH, 1, 6 * D))


def _ffn_kernel(x_ref, sh_ref, sc_ref, gt_ref, gpre_ref, gpost_ref, wa_ref, wb_ref, cwa_ref, cwb_ref,
                cba_ref, cbb_ref, wd_ref, *rest, split_out):
    outs, (h_scr, halo_scr, acc_scr) = rest[:-3], rest[-3:]
    i, j = pl.program_id(0), pl.program_id(1)

    @pl.when(j == 0)
    def _():
        h = _pre(x_ref[...], gpre_ref[...], sh_ref[...], sc_ref[...])
        h_scr[...] = h.astype(BF16)
        halo_scr[...] = _halo_rows(h, i)
        acc_scr[...] = jnp.zeros_like(acc_scr)

    def project(stage):
        s, (c0, cw) = stage
        h = h_scr[s * SEG:(s + 1) * SEG, :]
        cols = slice(c0, c0 + cw)
        return tuple(jnp.dot(lhs, w[:, cols], preferred_element_type=F32)
                     for w in (wa_ref, wb_ref) for lhs in (h, halo_scr[...]))

    def consume(stage, up):
        s, (c0, cw) = stage
        cols = slice(c0, c0 + cw)
        a = _conv3(up[0], cwa_ref[:, cols], up[1], s) + cba_ref[:, cols]
        b = _conv3(up[2], cwb_ref[:, cols], up[3], s) + cbb_ref[:, cols]
        act = (_silu(a) * b).astype(BF16)
        acc_scr[s * SEG:(s + 1) * SEG, :] += jnp.dot(act, wd_ref[cols, :], preferred_element_type=F32)

    _pipelined([(s, t) for s in range(N_SEG) for t in _sub_tiles(FFN_TN)], project, consume)

    last = j == pl.num_programs(1) - 1
    if split_out:
        @pl.when(jnp.logical_and(last, i < PROMPT_BLKS))
        def _():
            outs[0][...] = x_ref[...] + gt_ref[...] * _rms(acc_scr[...], gpost_ref[...])

        @pl.when(jnp.logical_and(last, i >= PROMPT_BLKS))
        def _():
            outs[1][...] = x_ref[...] + gt_ref[...] * _rms(acc_scr[...], gpost_ref[...])
    else:
        @pl.when(last)
        def _():
            outs[0][...] = x_ref[...] + gt_ref[...] * _rms(acc_scr[...], gpost_ref[...])


def _ffn_layer(x, mod, gains, layer, w_up, conv_w, conv_b, w_down, split_out=False):
    nj = FFN // FFN_TN
    row = pl.BlockSpec((ROW_BLK, D), lambda i, j: (i, 0))
    if split_out:
        out_specs = _split_specs(D)
        out_shape = [jax.ShapeDtypeStruct((TOK_PROMPT, D), F32), jax.ShapeDtypeStruct((N_TOK - TOK_PROMPT, D), F32)]
    else:
        out_specs, out_shape = [row], [jax.ShapeDtypeStruct((N_TOK, D), F32)]
    res = pl.pallas_call(
        functools.partial(_ffn_kernel, split_out=split_out),
        grid=(N_BLKS, nj),
        in_specs=[
            row, _mod_spec(layer, 3), _mod_spec(layer, 4), _mod_spec(layer, 5),
            _gain_spec(layer, 2), _gain_spec(layer, 3),
            pl.BlockSpec((D, FFN_TN), lambda i, j: (0, j)),
            pl.BlockSpec((D, FFN_TN), lambda i, j: (0, nj + j)),
            pl.BlockSpec((3, FFN_TN), lambda i, j: (0, j)),
            pl.BlockSpec((3, FFN_TN), lambda i, j: (0, nj + j)),
            pl.BlockSpec((1, FFN_TN), lambda i, j: (0, j)),
            pl.BlockSpec((1, FFN_TN), lambda i, j: (0, nj + j)),
            pl.BlockSpec((FFN_TN, D), lambda i, j: (j, 0)),
        ],
        out_specs=out_specs,
        out_shape=out_shape,
        scratch_shapes=[pltpu.VMEM((ROW_BLK, D), BF16), pltpu.VMEM((HALO_ROWS, D), BF16),
                        pltpu.VMEM((ROW_BLK, D), F32)],
        compiler_params=pltpu.CompilerParams(
            dimension_semantics=("arbitrary", "arbitrary"), vmem_limit_bytes=VMEM_LIMIT),
        name="conv_ffn",
    )(x, mod, mod, mod, gains, gains, w_up, w_up, conv_w, conv_w, conv_b, conv_b, w_down)
    return tuple(res) if split_out else res[0]


def _out_kernel(yp_ref, ys_ref, w_ref, *rest, split_x):
    i = pl.program_id(0)
    x_refs, (gt_ref, g_ref, o_ref) = rest[:-3], rest[-3:]

    def run(y_ref, x_ref):
        def project(s):
            return jnp.dot(y_ref[s * SEG:(s + 1) * SEG, :], w_ref[...], preferred_element_type=F32)

        def consume(s, y):
            rows = slice(s * SEG, (s + 1) * SEG)
            o_ref[rows, :] = x_ref[rows, :] + gt_ref[...] * _rms(y, g_ref[...])

        _pipelined(list(range(N_SEG)), project, consume)

    @pl.when(i < PROMPT_BLKS)
    def _():
        run(yp_ref, x_refs[0])

    @pl.when(i >= PROMPT_BLKS)
    def _():
        run(ys_ref, x_refs[-1])


def _out_layer(y_prompt, y_latent, w, x, mod, gains, layer):
    split_x = isinstance(x, tuple)
    row = pl.BlockSpec((ROW_BLK, D), lambda i: (i, 0))
    return pl.pallas_call(
        functools.partial(_out_kernel, split_x=split_x),
        grid=(N_BLKS,),
        in_specs=_split_specs(D) + [pl.BlockSpec((D, D), lambda i: (0, 0))]
        + (_split_specs(D) if split_x else [row]) + [_mod_spec(layer, 2), _gain_spec(layer, 1)],
        out_specs=row,
        out_shape=jax.ShapeDtypeStruct((N_TOK, D), F32),
        compiler_params=pltpu.CompilerParams(
            dimension_semantics=("arbitrary",), vmem_limit_bytes=VMEM_LIMIT),
        name="mixer_out",
    )(y_prompt, y_latent, w, *(x if split_x else (x,)), mod, gains)


def _dn_in_kernel(*refs, split_x):
    x_refs, refs = (refs[:2], refs[2:]) if split_x else (refs[:1], refs[1:])
    sh_ref, sc_ref, gpre_ref, w_ref, cw_ref, wab_ref, alog_ref, dtb_ref, p_ref, g_ref, h_scr, halo_scr = refs
    i, j = pl.program_id(0), pl.program_id(1)

    @pl.when(j == 0)
    def _():
        x = _pick(i, *x_refs) if split_x else x_refs[0][...]
        hf = _pre(x, gpre_ref[...], sh_ref[...], sc_ref[...])
        h = hf.astype(BF16)
        h_scr[...] = h
        halo_scr[...] = _halo_rows(hf, i)
        ab = jnp.dot(h, wab_ref[...], preferred_element_type=F32)
        t = ab + dtb_ref[...]
        softplus = jnp.maximum(t, 0.0) + jnp.log(1.0 + jnp.exp(-jnp.abs(t)))
        lane = lax.broadcasted_iota(jnp.int32, ab.shape, 1)
        g_ref[...] = jnp.where(lane < 2 * HEADS_DN, -jnp.exp(alog_ref[...]) * softplus, jax.nn.sigmoid(ab))

    stages = [(s, t) for s in range(N_SEG) for t in _sub_tiles(DN_TN)]

    def project(stage, with_halo=True):
        s, (c0, cw) = stage
        w = w_ref[:, c0:c0 + cw]
        u = jnp.dot(h_scr[s * SEG:(s + 1) * SEG, :], w, preferred_element_type=F32)
        return (u, jnp.dot(halo_scr[...], w, preferred_element_type=F32)) if with_halo else u

    def activated(stage, up):
        s, (c0, cw) = stage
        return _silu(_conv3(up[0], cw_ref[:, c0:c0 + cw], up[1], s))

    @pl.when(j < 2)
    def _():
        qscale = jnp.where(j == 0, DK ** -0.5, 1.0)

        def consume(stage, up):
            s, (c0, cw) = stage
            act = activated(stage, up)
            for t in range(cw // DK):
                xs = act[:, t * DK:(t + 1) * DK]
                inv = lax.rsqrt(jnp.sum(xs * xs, axis=-1, keepdims=True) + EPS) * qscale
                p_ref[s * SEG:(s + 1) * SEG, c0 + t * DK:c0 + (t + 1) * DK] = (xs * inv).astype(BF16)

        _pipelined(stages, project, consume)

    @pl.when(j == 2)
    def _():
        def consume(stage, up):
            s, (c0, cw) = stage
            p_ref[s * SEG:(s + 1) * SEG, c0:c0 + cw] = activated(stage, up).astype(BF16)

        _pipelined(stages, project, consume)

    @pl.when(j == 3)
    def _():
        def consume(stage, u):
            s, (c0, cw) = stage
            p_ref[s * SEG:(s + 1) * SEG, c0:c0 + cw] = u.astype(BF16)

        _pipelined(stages, functools.partial(project, with_halo=False), consume)


def _dn_in_layer(x, mod, gains, layer, w_main, conv_w, w_ab, a_log, dt_bias):
    split_x = isinstance(x, tuple)
    row = pl.BlockSpec((ROW_BLK, D), lambda i, j: (i, 0))
    vec = pl.BlockSpec((1, LANES), lambda i, j: (0, 0))
    return pl.pallas_call(
        functools.partial(_dn_in_kernel, split_x=split_x),
        grid=(N_BLKS, DN_MAIN // DN_TN),
        in_specs=(_split_specs(D) if split_x else [row]) + [
            _mod_spec(layer, 0), _mod_spec(layer, 1), _gain_spec(layer, 0),
            pl.BlockSpec((D, DN_TN), lambda i, j: (0, j)),
            pl.BlockSpec((3, DN_TN), lambda i, j: (0, j)),
            pl.BlockSpec((D, LANES), lambda i, j: (0, 0)),
            vec, vec,
        ],
        out_specs=[
            pl.BlockSpec((ROW_BLK, DN_TN), lambda i, j: (i, j)),
            pl.BlockSpec((ROW_BLK, LANES), lambda i, j: (i, 0)),
        ],
        out_shape=[jax.ShapeDtypeStruct((N_TOK, DN_MAIN), BF16), jax.ShapeDtypeStruct((N_TOK, LANES), F32)],
        scratch_shapes=[pltpu.VMEM((ROW_BLK, D), BF16), pltpu.VMEM((HALO_ROWS, D), BF16)],
        compiler_params=pltpu.CompilerParams(
            dimension_semantics=("arbitrary", "arbitrary"), vmem_limit_bytes=VMEM_LIMIT),
        name="deltanet_in",
    )(*(x if split_x else (x,)), mod, mod, gains, w_main, conv_w, w_ab, a_log, dt_bias)


def _bdot_nt(a, b):
    return lax.dot_general(a.astype(BF16), b.astype(BF16), (((1,), (1,)), ((), ())), preferred_element_type=F32)


def _bdot_tn(a, b):
    return lax.dot_general(a.astype(BF16), b.astype(BF16), (((0,), (0,)), ((), ())), preferred_element_type=F32)


def _bmm(a, b):
    return jnp.einsum('gmk,gkn->gmn', a.astype(BF16), b.astype(BF16), preferred_element_type=F32)


def _bmm_nt(a, b):
    return jnp.einsum('gmk,gnk->gmn', a.astype(BF16), b.astype(BF16), preferred_element_type=F32)


SOLVE_BLK = 16


def _invert_diagonal_blocks(tri, reverse):
    G, C, nb = tri.shape[0], CHUNK, CHUNK // SOLVE_BLK
    half = G // 2
    pair = jnp.concatenate([tri[:half], tri[half:]], axis=2)
    r = lax.broadcasted_iota(jnp.int32, (C, LANES), 0)
    c = lax.broadcasted_iota(jnp.int32, (C, LANES), 1) % C
    diag = jnp.where(r // SOLVE_BLK == c // SOLVE_BLK, pair, jnp.zeros_like(pair))
    kk = lax.broadcasted_iota(jnp.int32, (LANES, SOLVE_BLK * LANES), 0)
    nn = lax.broadcasted_iota(jnp.int32, (LANES, SOLVE_BLK * LANES), 1)
    pick = jnp.where((kk % SOLVE_BLK == nn // LANES) & (kk // C == (nn % LANES) // C), 1.0, 0.0).astype(BF16)
    cols = jnp.einsum('gik,kn->gin', diag, pick, preferred_element_type=F32)
    inv_d = jnp.broadcast_to(jnp.where(r == c, 1.0, 0.0), (half, C, LANES))
    steps = range(SOLVE_BLK - 1, 0, -1) if reverse else range(SOLVE_BLK - 1)
    for s in steps:
        pivot_rows = jnp.concatenate(
            [jnp.broadcast_to(inv_d[:, b * SOLVE_BLK + s:b * SOLVE_BLK + s + 1, :], (half, SOLVE_BLK, LANES))
             for b in range(nb)], axis=1)
        inv_d = inv_d - cols[:, :, s * LANES:(s + 1) * LANES] * pivot_rows
    inv_d = inv_d.astype(BF16)
    return jnp.concatenate([inv_d[:, :, :C], inv_d[:, :, C:]], axis=0)


def _block_substitution(systems):
    C, nb = CHUNK, CHUNK // SOLVE_BLK
    r = lax.broadcasted_iota(jnp.int32, (C, C), 0)
    c = lax.broadcasted_iota(jnp.int32, (C, C), 1)
    offs = [jnp.where(r // SOLVE_BLK != c // SOLVE_BLK, tri, jnp.zeros_like(tri)) for tri, _, _, _ in systems]

    def placed(block, b):
        G, _, width = block.shape
        parts = []
        if b > 0:
            parts.append(jnp.zeros((G, b * SOLVE_BLK, width), F32))
        parts.append(block)
        if b < nb - 1:
            parts.append(jnp.zeros((G, (nb - 1 - b) * SOLVE_BLK, width), F32))
        return jnp.concatenate(parts, axis=1)

    xs = [None] * len(systems)
    for step in range(nb):
        ys = []
        for n, (_, _, rhs, reverse) in enumerate(systems):
            b = nb - 1 - step if reverse else step
            rows = slice(b * SOLVE_BLK, (b + 1) * SOLVE_BLK)
            ys.append(rhs[:, rows] if xs[n] is None else rhs[:, rows] - _bmm(offs[n][:, rows], xs[n]))
        for n, (_, inv_d, _, reverse) in enumerate(systems):
            b = nb - 1 - step if reverse else step
            rows = slice(b * SOLVE_BLK, (b + 1) * SOLVE_BLK)
            xb = placed(_bmm(inv_d[:, rows], placed(ys[n], b)), b)
            xs[n] = xb if xs[n] is None else xs[n] + xb
    return xs


def _dn_chunk_terms(q, k, v, g_raw, beta, reverse):
    G, C = q.shape[0], CHUNK
    r = lax.broadcasted_iota(jnp.int32, (C, C), 0)
    c = lax.broadcasted_iota(jnp.int32, (C, C), 1)
    incl = (r <= c) if reverse else (r >= c)
    strict = (r < c) if reverse else (r > c)
    g_mat = jnp.broadcast_to(g_raw, (G, C, C))
    scanned = (r >= c) if reverse else (r <= c)
    cum_row = jnp.sum(jnp.where(scanned, g_mat, 0.0), axis=1, keepdims=True)
    cum_col = jnp.sum(jnp.where(r == c, jnp.broadcast_to(cum_row, (G, C, C)), 0.0), axis=2, keepdims=True)
    decay = jnp.exp(jnp.where(incl, cum_col - cum_row, -jnp.inf))
    kb = k * beta
    kk = _bmm_nt(kb, k) * decay
    qk = _bmm_nt(q, k) * decay
    tri = jnp.where(strict, kk, 0.0).astype(BF16)
    e_col = jnp.exp(cum_col)
    rhs = jnp.concatenate([v * beta, kb * e_col], axis=2)
    total = cum_col[:, 0:1] if reverse else cum_col[:, C - 1:C]
    k_end = k * jnp.exp(total - cum_col)
    return tri, rhs, q * e_col, k_end, qk, jnp.exp(total)


PREP_BATCH = 32


def _dn_scan_kernel(q_ref, k_ref, v_ref, z_ref, g_ref, s0_ref, on_ref, y_ref, *rest, seq_len, with_final, HG):
    if with_final:
        sfin_ref, rest = rest[0], rest[1:]
    u_scr, w_scr, qe_scr, ke_scr, qk_scr, el_scr, o_scr, s_scr = rest
    n = seq_len // CHUNK
    G = PREP_BATCH // HG
    head0 = pl.program_id(1) * HG
    lane = lax.broadcasted_iota(jnp.int32, (G, CHUNK, LANES), 2)

    def prepare(grp, carry):
        rows = pl.ds(pl.multiple_of(grp * (G * CHUNK), G * CHUNK), G * CHUNK)

        def by_head(ref):
            x = ref[rows, :].astype(F32)
            return jnp.concatenate([x[:, hh * DK:(hh + 1) * DK].reshape(G, CHUNK, DK) for hh in range(HG)], axis=0)

        q, k, v = by_head(q_ref), by_head(k_ref), by_head(v_ref)
        gates = g_ref[rows, :].reshape(G, CHUNK, LANES)

        def gate(col):
            return jnp.concatenate(
                [jnp.sum(jnp.where(lane == col + head0 + hh, gates, 0.0), axis=2, keepdims=True) for hh in range(HG)],
                axis=0)

        terms = [_dn_chunk_terms(q, k, v, gate(d * HEADS_DN), gate((2 + d) * HEADS_DN), reverse=(d == 1))
                 for d in range(2)]
        inverses = [_invert_diagonal_blocks(terms[d][0], reverse=(d == 1)) for d in range(2)]
        sols = _block_substitution([(terms[d][0], inverses[d], terms[d][1], d == 1) for d in range(2)])
        for d in range(2):
            _, _, qe, ke, qk, el = terms[d]
            for hh in range(HG):
                part = slice(hh * G, (hh + 1) * G)
                u_scr[d, hh, rows, :] = sols[d][part, :, :DK].reshape(G * CHUNK, DK)
                w_scr[d, hh, rows, :] = sols[d][part, :, DK:].reshape(G * CHUNK, DK).astype(BF16)
                qe_scr[d, hh, rows, :] = qe[part].reshape(G * CHUNK, DK).astype(BF16)
                ke_scr[d, hh, rows, :] = ke[part].reshape(G * CHUNK, DK).astype(BF16)
                qk_scr[d, hh, rows, :] = qk[part].reshape(G * CHUNK, CHUNK).astype(BF16)
                el_scr[d, hh, pl.ds(grp * G, G)] = jnp.broadcast_to(el[part], (G, 8, LANES))
        return carry

    lax.fori_loop(0, n // G, prepare, 0)
    s_scr[...] = s0_ref[...]

    def scan(t, carry):
        chains = [(hh, d) for hh in range(HG) for d in range(2)]
        rows = [pl.ds(pl.multiple_of((t if d == 0 else n - 1 - t) * CHUNK, CHUNK), CHUNK) for _, d in chains]
        states = [s_scr[d, hh] for hh, d in chains]
        from_state = []
        for (hh, d), r, s in zip(chains, rows, states):
            sb = s.astype(BF16)
            from_state.append((jnp.dot(w_scr[d, hh, r, :], sb, preferred_element_type=F32),
                               jnp.dot(qe_scr[d, hh, r, :], sb, preferred_element_type=F32)))
        for (hh, d), r, s, (ws, qs) in zip(chains, rows, states, from_state):
            c = t if d == 0 else n - 1 - t
            vb = (u_scr[d, hh, r, :] - ws).astype(BF16)
            o_scr[d, hh, r, :] = qs + jnp.dot(qk_scr[d, hh, r, :], vb, preferred_element_type=F32)
            s_scr[d, hh] = s * el_scr[d, hh, c][0:1, :] + _bdot_tn(ke_scr[d, hh, r, :], vb)
        return carry

    lax.fori_loop(0, n, scan, 0)
    if with_final:
        sfin_ref[...] = s_scr[...]
    for hh in range(HG):
        cols = slice(hh * DK, (hh + 1) * DK)
        o = o_scr[0, hh] + o_scr[1, hh]
        y_ref[:, cols] = (_rms(o, on_ref[...]) * _silu(z_ref[:, cols].astype(F32))).astype(BF16)


def _dn_scan(p, gates, s0, s0_index, onorm_g, *, n_seq, seq_len, row_offset, with_final):
    blk0 = row_offset // seq_len
    n = seq_len // CHUNK
    HG = max(PREP_BATCH // n, HEADS_DN // 2)
    n_grp = HEADS_DN // HG

    def part(k):
        return pl.BlockSpec((seq_len, HG * DK), lambda b, h: (blk0 + b, k * n_grp + h))

    out_specs = [pl.BlockSpec((seq_len, HG * DK), lambda b, h: (b, h))]
    out_shape = [jax.ShapeDtypeStruct((n_seq * seq_len, D), BF16)]
    if with_final:
        out_specs.append(pl.BlockSpec((None, 2, HG, DK, DK), lambda b, h: (b, 0, h, 0, 0)))
        out_shape.append(jax.ShapeDtypeStruct((n_seq, 2, HEADS_DN, DK, DK), F32))
    n = seq_len // CHUNK
    res = pl.pallas_call(
        functools.partial(_dn_scan_kernel, seq_len=seq_len, with_final=with_final, HG=HG),
        grid=(n_seq, n_grp),
        in_specs=[
            part(0), part(1), part(2), part(3),
            pl.BlockSpec((seq_len, LANES), lambda b, h: (blk0 + b, 0)),
            pl.BlockSpec((None,) * (s0.ndim - 4) + (2, HG, DK, DK), s0_index),
            pl.BlockSpec((1, DK), lambda b, h: (0, 0)),
        ],
        out_specs=out_specs,
        out_shape=out_shape,
        scratch_shapes=[
            pltpu.VMEM((2, HG, seq_len, DK), F32),
            pltpu.VMEM((2, HG, seq_len, DK), BF16),
            pltpu.VMEM((2, HG, seq_len, DK), BF16),
            pltpu.VMEM((2, HG, seq_len, DK), BF16),
            pltpu.VMEM((2, HG, seq_len, CHUNK), BF16),
            pltpu.VMEM((2, HG, n, 8, LANES), F32),
            pltpu.VMEM((2, HG, seq_len, DK), F32),
            pltpu.VMEM((2, HG, DK, DK), F32),
        ],
        compiler_params=pltpu.CompilerParams(
            dimension_semantics=("arbitrary", "arbitrary"), vmem_limit_bytes=VMEM_LIMIT),
        name="deltanet_scan_%d" % seq_len,
    )(p, p, p, p, gates, s0, onorm_g)
    return res


def _rope_tile(x, cos, sin):
    lane = lax.broadcasted_iota(jnp.int32, x.shape, 1)
    width = x.shape[1]
    partner = jnp.where((lane & 31) < 16, pltpu.roll(x, width - 16, 1), pltpu.roll(x, 16, 1))
    return x * cos + partner * sin


def _at_in_kernel(x_ref, sh_ref, sc_ref, gpre_ref, w_ref, cos_ref, sin_ref, q_ref, kv_ref, h_scr):
    i = pl.program_id(0)
    h_scr[...] = _pre(x_ref[...], gpre_ref[...], sh_ref[...], sc_ref[...]).astype(BF16)
    stages = [(s, t) for s in range(N_SEG) for t in _sub_tiles(AT_Q + 2 * AT_KVW)]

    def project(stage):
        s, (c0, cw) = stage
        return jnp.dot(h_scr[s * SEG:(s + 1) * SEG, :], w_ref[:, c0:c0 + cw], preferred_element_type=F32)

    def consume(stage, u, rope):
        s, (c0, cw) = stage
        rows = slice(s * SEG, (s + 1) * SEG)
        if rope and c0 < AT_Q + AT_KVW:
            u = jnp.concatenate([_rope_tile(u[:, t * LANES:(t + 1) * LANES], cos_ref[rows, :], sin_ref[rows, :])
                                 for t in range(cw // LANES)], axis=1)
        if c0 < AT_Q:
            q_ref[rows, c0:c0 + cw] = (u * Q_SCALE).astype(BF16)
        else:
            kv_ref[rows, c0 - AT_Q:c0 - AT_Q + cw] = u

    @pl.when(i >= PROMPT_BLKS)
    def _():
        _pipelined(stages, project, functools.partial(consume, rope=True))

    @pl.when(i < PROMPT_BLKS)
    def _():
        _pipelined(stages, project, functools.partial(consume, rope=False))


def _at_in_layer(x, mod, gains, layer, w_qkv, cos, sin):
    width = AT_Q + 2 * AT_KVW
    row = pl.BlockSpec((ROW_BLK, D), lambda i: (i, 0))
    tab = pl.BlockSpec((ROW_BLK, LANES), lambda i: (0, 0))
    return pl.pallas_call(
        _at_in_kernel,
        grid=(N_BLKS,),
        in_specs=[row, _mod_spec(layer, 0), _mod_spec(layer, 1), _gain_spec(layer, 0),
                  pl.BlockSpec((D, width), lambda i: (0, 0)), tab, tab],
        out_specs=[pl.BlockSpec((ROW_BLK, AT_Q), lambda i: (i, 0)),
                   pl.BlockSpec((ROW_BLK, 2 * AT_KVW), lambda i: (i, 0))],
        out_shape=[jax.ShapeDtypeStruct((N_TOK, AT_Q), BF16), jax.ShapeDtypeStruct((N_TOK, 2 * AT_KVW), F32)],
        scratch_shapes=[pltpu.VMEM((ROW_BLK, D), BF16)],
        compiler_params=pltpu.CompilerParams(
            dimension_semantics=("arbitrary",), vmem_limit_bytes=VMEM_LIMIT),
        name="attention_in",
    )(x, mod, mod, gains, w_qkv, cos, sin)


def _softmax_pv(scores, values, sink):
    m = sink
    for s in scores:
        m = jnp.maximum(m, jnp.max(s, axis=0, keepdims=True))
    denom = jnp.exp2(sink - m)
    acc = None
    for s, v in zip(scores, values):
        p = jnp.exp2(s - m)
        denom = denom + jnp.sum(p, axis=0, keepdims=True)
        pv = _bdot_tn(v, p)
        acc = pv if acc is None else acc + pv
    return acc / denom


def _attend_groups(q_ref, scores_of, values_of, sink_ref):
    rows = q_ref.shape[0]

    def stacked_q(kvh):
        return jnp.concatenate([q_ref[:, (kvh * AT_G + g) * HD:(kvh * AT_G + g + 1) * HD] for g in range(AT_G)], axis=0)

    def stacked_sink(kvh):
        return jnp.concatenate([jnp.broadcast_to(sink_ref[0:1, kvh * AT_G + g:kvh * AT_G + g + 1], (1, rows))
                                for g in range(AT_G)], axis=1)

    outs = []
    pending = scores_of(stacked_q(0), 0)
    for kvh in range(AT_KV):
        upcoming = scores_of(stacked_q(kvh + 1), kvh + 1) if kvh + 1 < AT_KV else None
        o = _softmax_pv(pending, values_of(kvh), stacked_sink(kvh))
        outs.extend(o[:, g * rows:(g + 1) * rows] for g in range(AT_G))
        pending = upcoming
    return jnp.concatenate(outs, axis=0).T.astype(BF16)


def _ctx_attn_kernel(q_ref, kv_ref, sink_ref, y_ref):
    keys = [kv_ref[:, kvh * HD:(kvh + 1) * HD].astype(BF16) for kvh in range(AT_KV)]
    vals = [kv_ref[:, AT_KVW + kvh * HD:AT_KVW + (kvh + 1) * HD].astype(BF16) for kvh in range(AT_KV)]
    y_ref[...] = _attend_groups(q_ref, lambda q, kvh: [_bdot_nt(keys[kvh], q)], lambda kvh: [vals[kvh]], sink_ref)


def _ctx_attention(q, kv, sink):
    nq = L_PROMPT // QBLK
    return pl.pallas_call(
        _ctx_attn_kernel,
        grid=(N_PROMPT, nq),
        in_specs=[
            pl.BlockSpec((QBLK, AT_Q), lambda b, t: (b * nq + t, 0)),
            pl.BlockSpec((L_PROMPT, 2 * AT_KVW), lambda b, t: (b, 0)),
            pl.BlockSpec((1, LANES), lambda b, t: (0, 0)),
        ],
        out_specs=pl.BlockSpec((QBLK, AT_Q), lambda b, t: (b * nq + t, 0)),
        out_shape=jax.ShapeDtypeStruct((TOK_PROMPT, AT_Q), BF16),
        compiler_params=pltpu.CompilerParams(dimension_semantics=("arbitrary", "arbitrary")),
        name="context_attention",
    )(q, kv, sink)


def _lat_attn_kernel(q_ref, kv_ref, ck_ref, cv_ref, sink_ref, y_ref):
    t = pl.program_id(1)
    k0 = pl.multiple_of(jnp.clip(t * QBLK - WINDOW, 0, L_LATENT - BAND), QBLK)
    kpos = k0 + lax.broadcasted_iota(jnp.int32, (BAND, QBLK), 0)
    qpos = t * QBLK + lax.broadcasted_iota(jnp.int32, (BAND, QBLK), 1)
    valid = jnp.concatenate([jnp.abs(qpos - kpos) <= WINDOW] * AT_G, axis=1)
    band = pl.ds(k0, BAND)
    k_loc = [kv_ref[band, kvh * HD:(kvh + 1) * HD].astype(BF16) for kvh in range(AT_KV)]
    v_loc = [kv_ref[band, AT_KVW + kvh * HD:AT_KVW + (kvh + 1) * HD].astype(BF16) for kvh in range(AT_KV)]
    k_ctx = [ck_ref[:, kvh * HD:(kvh + 1) * HD].astype(BF16) for kvh in range(AT_KV)]
    v_ctx = [cv_ref[:, kvh * HD:(kvh + 1) * HD].astype(BF16) for kvh in range(AT_KV)]

    def scores_of(q, kvh):
        return [jnp.where(valid, _bdot_nt(k_loc[kvh], q), -jnp.inf), _bdot_nt(k_ctx[kvh], q)]

    y_ref[...] = _attend_groups(q_ref, scores_of, lambda kvh: [v_loc[kvh], v_ctx[kvh]], sink_ref)


def _lat_attention(q, kv, cache_k, cache_v, sink, attn_layer):
    nq = L_LATENT // QBLK
    q0 = TOK_PROMPT // QBLK
    s0 = TOK_PROMPT // L_LATENT
    cache = pl.BlockSpec((None, None, PAST, AT_KVW), lambda b, t: (b, attn_layer, 0, 0))
    return pl.pallas_call(
        _lat_attn_kernel,
        grid=(N_LATENT, nq),
        in_specs=[
            pl.BlockSpec((QBLK, AT_Q), lambda b, t: (q0 + b * nq + t, 0)),
            pl.BlockSpec((L_LATENT, 2 * AT_KVW), lambda b, t: (s0 + b, 0)),
            cache, cache,
            pl.BlockSpec((1, LANES), lambda b, t: (0, 0)),
        ],
        out_specs=pl.BlockSpec((QBLK, AT_Q), lambda b, t: (b * nq + t, 0)),
        out_shape=jax.ShapeDtypeStruct((N_LATENT * L_LATENT, AT_Q), BF16),
        compiler_params=pltpu.CompilerParams(dimension_semantics=("arbitrary", "arbitrary")),
        name="latent_attention",
    )(q, kv, cache_k, cache_v, sink)


def _rope_tables():
    pos = jnp.arange(L_LATENT)
    half = HD // 2
    inv = jnp.power(ROPE_BASE, -jnp.arange(0, half, 2, dtype=F32) / half)
    ang_row = (pos // GRID_W).astype(F32)[:, None] * inv
    ang_col = (pos % GRID_W).astype(F32)[:, None] * inv
    ang = jnp.concatenate([ang_row, ang_row, ang_col, ang_col], axis=1)
    sign = jnp.tile(jnp.concatenate([-jnp.ones((16,), F32), jnp.ones((16,), F32)]), 2)
    cos = jnp.tile(jnp.cos(ang), (1, 2))
    sin = jnp.tile(jnp.sin(ang) * sign, (1, 2))
    return cos, sin


def _pad_lanes(v):
    v = v.reshape(1, -1).astype(F32)
    return jnp.pad(v, ((0, 0), (0, LANES - v.shape[1])))


def kernel(x_prompt, x_sample, state_delta, cache_k, cache_v, c, c_ctx, w_ada, b_ada, norm_g, dn_w_in, dn_conv_w, dn_a_log, dn_dt_bias, dn_onorm_g, dn_w_out, at_w_qkv, at_sink, at_w_o, ffn_w_up, ffn_conv_w, ffn_conv_b, ffn_w_down):
    x = (x_prompt.reshape(TOK_PROMPT, D), x_sample.reshape(N_TOK - TOK_PROMPT, D))
    cvecs = jnp.concatenate([c_ctx[None, :], c, jnp.zeros((MOD_ROWS - 1 - N_LATENT, D), F32)], axis=0)
    mod = _modulation(cvecs, w_ada, b_ada).reshape(DEPTH, MOD_ROWS, 1, 6 * D)
    gains = norm_g.reshape(DEPTH, 4, 1, D)
    cos, sin = _rope_tables()
    ck = cache_k.reshape(N_LATENT, -1, PAST, AT_KVW)
    cv = cache_v.reshape(N_LATENT, -1, PAST, AT_KVW)
    zero_state = jnp.zeros((2, HEADS_DN, DK, DK), F32)

    states, new_k, new_v = [], [], []
    for layer in range(DEPTH):
        j = layer // 2
        if layer % 2 == 0:
            w_in = dn_w_in[j]
            w_main = w_in[:, :DN_MAIN].astype(BF16)
            w_ab = jnp.pad(w_in[:, DN_MAIN:], ((0, 0), (0, LANES - 4 * HEADS_DN))).astype(BF16)
            conv_w = jnp.pad(dn_conv_w[j], ((0, 0), (0, D)))
            p, gates = _dn_in_layer(x, mod, gains, layer, w_main, conv_w, w_ab,
                                    _pad_lanes(dn_a_log[j]), _pad_lanes(dn_dt_bias[j]))
            onorm = dn_onorm_g[j].reshape(1, DK)
            y_p, s_fin = _dn_scan(p, gates, zero_state, lambda b, h: (0, h, 0, 0), onorm,
                                  n_seq=N_PROMPT, seq_len=L_PROMPT, row_offset=0, with_final=True)
            (y_s,) = _dn_scan(p, gates, state_delta, lambda b, h, j=j: (b, j, 0, h, 0, 0), onorm,
                              n_seq=N_LATENT, seq_len=L_LATENT, row_offset=TOK_PROMPT, with_final=False)
            states.append(s_fin)
            w_o = dn_w_out[j]
        else:
            q, kv = _at_in_layer(x, mod, gains, layer, at_w_qkv[j].astype(BF16), cos, sin)
            sink = _pad_lanes(at_sink[j]) * LOG2E
            y_p = _ctx_attention(q, kv, sink)
            y_s = _lat_attention(q, kv, ck, cv, sink, j)
            new_k.append(kv[:TOK_PROMPT, :AT_KVW].reshape(N_PROMPT, L_PROMPT, AT_KV, HD))
            new_v.append(kv[:TOK_PROMPT, AT_KVW:].reshape(N_PROMPT, L_PROMPT, AT_KV, HD))
            w_o = at_w_o[j]
        x = _out_layer(y_p, y_s, w_o.astype(BF16), x, mod, gains, layer)
        x = _ffn_layer(x, mod, gains, layer, ffn_w_up[layer].astype(BF16), ffn_conv_w[layer],
                       ffn_conv_b[layer].reshape(1, 2 * FFN), ffn_w_down[layer].astype(BF16),
                       split_out=(layer == DEPTH - 1))

    y_prompt = x[0].reshape(N_PROMPT, L_PROMPT, D)
    y_sample = x[1].reshape(N_LATENT, L_LATENT, D)
    return (y_prompt, y_sample, jnp.stack(states, axis=1), jnp.stack(new_k, axis=1), jnp.stack(new_v, axis=1))
```

```python
import functools

import jax
import jax.numpy as jnp
from jax import lax
from jax.experimental import pallas as pl
from jax.experimental.pallas import tpu as pltpu

F32 = jnp.float32
BF16 = jnp.bfloat16

D = 1024
N_PROMPT, L_PROMPT = 16, 256
N_LATENT, L_LATENT = 8, 1024
PAST = 512
GRID_W = 64
DEPTH = 4
HEADS_DN, DK = 8, 128
CHUNK = 64
DN_MAIN = 4 * D
AT_HEADS, AT_KV, AT_G, HD = 16, 4, 4, 64
AT_Q = AT_HEADS * HD
AT_KVW = AT_KV * HD
WINDOW = 128
QBLK = 128
BAND = QBLK + 2 * WINDOW
FFN = 2816
EPS = 1e-6
ROPE_BASE = 10000.0
LOG2E = 1.4426950408889634
Q_SCALE = HD ** -0.5 * LOG2E

ROW_BLK = 1024
TOK_PROMPT = N_PROMPT * L_PROMPT
N_TOK = TOK_PROMPT + N_LATENT * L_LATENT
PROMPT_BLKS = TOK_PROMPT // ROW_BLK
N_BLKS = N_TOK // ROW_BLK
MOD_ROWS = 16
LANES = 128
FFN_TN = 1408
MXU_N = 256
DN_TN = 1024
VMEM_LIMIT = 56 * 1024 * 1024


def _mod_row(i):
    return jnp.where(i < PROMPT_BLKS, 0, i - (PROMPT_BLKS - 1))


def _mod_spec(layer, chunk):
    return pl.BlockSpec((None, None, 1, D), lambda i, *_: (layer, _mod_row(i), 0, chunk))


def _gain_spec(layer, k):
    return pl.BlockSpec((None, None, 1, D), lambda i, *_: (layer, k, 0, 0))


def _split_specs(width):
    return [pl.BlockSpec((ROW_BLK, width), lambda i, *_: (jnp.minimum(i, PROMPT_BLKS - 1), 0)),
            pl.BlockSpec((ROW_BLK, width), lambda i, *_: (jnp.maximum(i - PROMPT_BLKS, 0), 0))]


def _pick(i, prompt_ref, latent_ref):
    return jnp.where(i < PROMPT_BLKS, prompt_ref[...], latent_ref[...])


def _rms(x, g):
    return x * lax.rsqrt(jnp.mean(x * x, axis=-1, keepdims=True) + EPS) * g


def _pre(x, g, shift, scale):
    return _rms(x, g) * (1.0 + scale) + shift


def _silu(x):
    half = 0.5 * x
    return half + half * jnp.tanh(half)


SEG = L_PROMPT
N_SEG = ROW_BLK // SEG
HALO_ROWS = 16


def _halo_rows(h, i):
    inside = jnp.where(i < PROMPT_BLKS, 0.0, 1.0)
    picks = []
    for s in range(1, N_SEG):
        picks += [h[s * SEG - 1:s * SEG], h[s * SEG:s * SEG + 1]]
    picks.append(jnp.zeros((HALO_ROWS - len(picks), h.shape[1]), F32))
    return (jnp.concatenate(picks, axis=0) * inside).astype(BF16)


def _conv3(u, w, halo, s):
    sub = lax.broadcasted_iota(jnp.int32, (8, 1), 0)
    zero_row = jnp.zeros((1, u.shape[1]), F32)
    before = halo[2 * s - 2:2 * s - 1] if s > 0 else zero_row
    after = halo[2 * s + 1:2 * s + 2] if s < N_SEG - 1 else zero_row
    down = pltpu.roll(u, 1, 0)
    up = pltpu.roll(u, SEG - 1, 0)
    prev = jnp.concatenate([jnp.where(sub == 0, before, down[:8]), down[8:]], axis=0)
    nxt = jnp.concatenate([up[:SEG - 8], jnp.where(sub == 7, after, up[SEG - 8:])], axis=0)
    return prev * w[0:1] + u * w[1:2] + nxt * w[2:3]


def _pipelined(stages, project, consume, ahead=1):
    pending = [project(st) for st in stages[:ahead]]
    for n, stage in enumerate(stages):
        if n + ahead < len(stages):
            pending.append(project(stages[n + ahead]))
        consume(stage, pending.pop(0))


def _sub_tiles(width):
    tiles, off = [], 0
    while off < width:
        size = min(MXU_N, width - off)
        tiles.append((off, size))
        off += size
    return tiles


def _mod_kernel(c_ref, w_ref, b_ref, o_ref):
    s = _silu(c_ref[...]).astype(BF16)
    o_ref[...] = jnp.dot(s, w_ref[...].astype(BF16), preferred_element_type=F32) + b_ref[...]


def _modulation(cvecs, w_ada, b_ada):
    tn = 1536
    return pl.pallas_call(
        _mod_kernel,
        grid=(DEPTH, 6 * D // tn),
        in_specs=[
            pl.BlockSpec((MOD_ROWS, D), lambda l, j: (0, 0)),
            pl.BlockSpec((None, D, tn), lambda l, j: (l, 0, j)),
            pl.BlockSpec((None, 1, tn), lambda l, j: (l, 0, j)),
        ],
        out_specs=pl.BlockSpec((None, MOD_ROWS, tn), lambda l, j: (l, 0, j)),
        out_shape=jax.ShapeDtypeStruct((DEPTH, MOD_ROWS, 6 * D), F32),
        compiler_params=pltpu.CompilerParams(
            dimension_semantics=("arbitrary", "arbitrary"), vmem_limit_bytes=VMEM_LIMIT),
        name="modulation",
    )(cvecs, w_ada, b_ada.reshape(DEPTH, 1, 6 * D))


def _ffn_kernel(x_ref, sh_ref, sc_ref, gt_ref, gpre_ref, gpost_ref, wa_ref, wb_ref, cwa_ref, cwb_ref,
                cba_ref, cbb_ref, wd_ref, *rest, split_out):
    outs, (h_scr, halo_scr, acc_scr) = rest[:-3], rest[-3:]
    i, j = pl.program_id(0), pl.program_id(1)

    @pl.when(j == 0)
    def _():
        h = _pre(x_ref[...], gpre_ref[...], sh_ref[...], sc_ref[...])
        h_scr[...] = h.astype(BF16)
        halo_scr[...] = _halo_rows(h, i)
        acc_scr[...] = jnp.zeros_like(acc_scr)

    def project(stage):
        s, (c0, cw) = stage
        h = h_scr[s * SEG:(s + 1) * SEG, :]
        cols = slice(c0, c0 + cw)
        return tuple(jnp.dot(lhs, w[:, cols], preferred_element_type=F32)
                     for w in (wa_ref, wb_ref) for lhs in (h, halo_scr[...]))

    def consume(stage, up):
        s, (c0, cw) = stage
        cols = slice(c0, c0 + cw)
        a = _conv3(up[0], cwa_ref[:, cols], up[1], s) + cba_ref[:, cols]
        b = _conv3(up[2], cwb_ref[:, cols], up[3], s) + cbb_ref[:, cols]
        act = (_silu(a) * b).astype(BF16)
        acc_scr[s * SEG:(s + 1) * SEG, :] += jnp.dot(act, wd_ref[cols, :], preferred_element_type=F32)

    _pipelined([(s, t) for s in range(N_SEG) for t in _sub_tiles(FFN_TN)], project, consume)

    last = j == pl.num_programs(1) - 1
    if split_out:
        @pl.when(jnp.logical_and(last, i < PROMPT_BLKS))
        def _():
            outs[0][...] = x_ref[...] + gt_ref[...] * _rms(acc_scr[...], gpost_ref[...])

        @pl.when(jnp.logical_and(last, i >= PROMPT_BLKS))
        def _():
            outs[1][...] = x_ref[...] + gt_ref[...] * _rms(acc_scr[...], gpost_ref[...])
    else:
        @pl.when(last)
        def _():
            outs[0][...] = x_ref[...] + gt_ref[...] * _rms(acc_scr[...], gpost_ref[...])


def _ffn_layer(x, mod, gains, layer, w_up, conv_w, conv_b, w_down, split_out=False):
    nj = FFN // FFN_TN
    row = pl.BlockSpec((ROW_BLK, D), lambda i, j: (i, 0))
    if split_out:
        out_specs = _split_specs(D)
        out_shape = [jax.ShapeDtypeStruct((TOK_PROMPT, D), F32), jax.ShapeDtypeStruct((N_TOK - TOK_PROMPT, D), F32)]
    else:
        out_specs, out_shape = [row], [jax.ShapeDtypeStruct((N_TOK, D), F32)]
    res = pl.pallas_call(
        functools.partial(_ffn_kernel, split_out=split_out),
        grid=(N_BLKS, nj),
        in_specs=[
            row, _mod_spec(layer, 3), _mod_spec(layer, 4), _mod_spec(layer, 5),
            _gain_spec(layer, 2), _gain_spec(layer, 3),
            pl.BlockSpec((D, FFN_TN), lambda i, j: (0, j)),
            pl.BlockSpec((D, FFN_TN), lambda i, j: (0, nj + j)),
            pl.BlockSpec((3, FFN_TN), lambda i, j: (0, j)),
            pl.BlockSpec((3, FFN_TN), lambda i, j: (0, nj + j)),
            pl.BlockSpec((1, FFN_TN), lambda i, j: (0, j)),
            pl.BlockSpec((1, FFN_TN), lambda i, j: (0, nj + j)),
            pl.BlockSpec((FFN_TN, D), lambda i, j: (j, 0)),
        ],
        out_specs=out_specs,
        out_shape=out_shape,
        scratch_shapes=[pltpu.VMEM((ROW_BLK, D), BF16), pltpu.VMEM((HALO_ROWS, D), BF16),
                        pltpu.VMEM((ROW_BLK, D), F32)],
        compiler_params=pltpu.CompilerParams(
            dimension_semantics=("arbitrary", "arbitrary"), vmem_limit_bytes=VMEM_LIMIT),
        name="conv_ffn",
    )(x, mod, mod, mod, gains, gains, w_up, w_up, conv_w, conv_w, conv_b, conv_b, w_down)
    return tuple(res) if split_out else res[0]


def _out_kernel(yp_ref, ys_ref, w_ref, *rest, split_x):
    i = pl.program_id(0)
    x_refs, (gt_ref, g_ref, o_ref) = rest[:-3], rest[-3:]

    def run(y_ref, x_ref):
        def project(s):
            return jnp.dot(y_ref[s * SEG:(s + 1) * SEG, :], w_ref[...], preferred_element_type=F32)

        def consume(s, y):
            rows = slice(s * SEG, (s + 1) * SEG)
            o_ref[rows, :] = x_ref[rows, :] + gt_ref[...] * _rms(y, g_ref[...])

        _pipelined(list(range(N_SEG)), project, consume)

    @pl.when(i < PROMPT_BLKS)
    def _():
        run(yp_ref, x_refs[0])

    @pl.when(i >= PROMPT_BLKS)
    def _():
        run(ys_ref, x_refs[-1])


def _out_layer(y_prompt, y_latent, w, x, mod, gains, layer):
    split_x = isinstance(x, tuple)
    row = pl.BlockSpec((ROW_BLK, D), lambda i: (i, 0))
    return pl.pallas_call(
        functools.partial(_out_kernel, split_x=split_x),
        grid=(N_BLKS,),
        in_specs=_split_specs(D) + [pl.BlockSpec((D, D), lambda i: (0, 0))]
        + (_split_specs(D) if split_x else [row]) + [_mod_spec(layer, 2), _gain_spec(layer, 1)],
        out_specs=row,
        out_shape=jax.ShapeDtypeStruct((N_TOK, D), F32),
        compiler_params=pltpu.CompilerParams(
            dimension_semantics=("arbitrary",), vmem_limit_bytes=VMEM_LIMIT),
        name="mixer_out",
    )(y_prompt, y_latent, w, *(x if split_x else (x,)), mod, gains)


def _dn_in_kernel(*refs, split_x):
    x_refs, refs = (refs[:2], refs[2:]) if split_x else (refs[:1], refs[1:])
    sh_ref, sc_ref, gpre_ref, w_ref, cw_ref, wab_ref, alog_ref, dtb_ref, p_ref, g_ref, h_scr, halo_scr = refs
    i, j = pl.program_id(0), pl.program_id(1)

    @pl.when(j == 0)
    def _():
        x = _pick(i, *x_refs) if split_x else x_refs[0][...]
        hf = _pre(x, gpre_ref[...], sh_ref[...], sc_ref[...])
        h = hf.astype(BF16)
        h_scr[...] = h
        halo_scr[...] = _halo_rows(hf, i)
        ab = jnp.dot(h, wab_ref[...], preferred_element_type=F32)
        t = ab + dtb_ref[...]
        softplus = jnp.maximum(t, 0.0) + jnp.log(1.0 + jnp.exp(-jnp.abs(t)))
        lane = lax.broadcasted_iota(jnp.int32, ab.shape, 1)
        g_ref[...] = jnp.where(lane < 2 * HEADS_DN, -jnp.exp(alog_ref[...]) * softplus, jax.nn.sigmoid(ab))

    stages = [(s, t) for s in range(N_SEG) for t in _sub_tiles(DN_TN)]

    def project(stage, with_halo=True):
        s, (c0, cw) = stage
        w = w_ref[:, c0:c0 + cw]
        u = jnp.dot(h_scr[s * SEG:(s + 1) * SEG, :], w, preferred_element_type=F32)
        return (u, jnp.dot(halo_scr[...], w, preferred_element_type=F32)) if with_halo else u

    def activated(stage, up):
        s, (c0, cw) = stage
        return _silu(_conv3(up[0], cw_ref[:, c0:c0 + cw], up[1], s))

    @pl.when(j < 2)
    def _():
        qscale = jnp.where(j == 0, DK ** -0.5, 1.0)

        def consume(stage, up):
            s, (c0, cw) = stage
            act = activated(stage, up)
            for t in range(cw // DK):
                xs = act[:, t * DK:(t + 1) * DK]
                inv = lax.rsqrt(jnp.sum(xs * xs, axis=-1, keepdims=True) + EPS) * qscale
                p_ref[s * SEG:(s + 1) * SEG, c0 + t * DK:c0 + (t + 1) * DK] = (xs * inv).astype(BF16)

        _pipelined(stages, project, consume)

    @pl.when(j == 2)
    def _():
        def consume(stage, up):
            s, (c0, cw) = stage
            p_ref[s * SEG:(s + 1) * SEG, c0:c0 + cw] = activated(stage, up).astype(BF16)

        _pipelined(stages, project, consume)

    @pl.when(j == 3)
    def _():
        def consume(stage, u):
            s, (c0, cw) = stage
            p_ref[s * SEG:(s + 1) * SEG, c0:c0 + cw] = u.astype(BF16)

        _pipelined(stages, functools.partial(project, with_halo=False), consume)


def _dn_in_layer(x, mod, gains, layer, w_main, conv_w, w_ab, a_log, dt_bias):
    split_x = isinstance(x, tuple)
    row = pl.BlockSpec((ROW_BLK, D), lambda i, j: (i, 0))
    vec = pl.BlockSpec((1, LANES), lambda i, j: (0, 0))
    return pl.pallas_call(
        functools.partial(_dn_in_kernel, split_x=split_x),
        grid=(N_BLKS, DN_MAIN // DN_TN),
        in_specs=(_split_specs(D) if split_x else [row]) + [
            _mod_spec(layer, 0), _mod_spec(layer, 1), _gain_spec(layer, 0),
            pl.BlockSpec((D, DN_TN), lambda i, j: (0, j)),
            pl.BlockSpec((3, DN_TN), lambda i, j: (0, j)),
            pl.BlockSpec((D, LANES), lambda i, j: (0, 0)),
            vec, vec,
        ],
        out_specs=[
            pl.BlockSpec((ROW_BLK, DN_TN), lambda i, j: (i, j)),
            pl.BlockSpec((ROW_BLK, LANES), lambda i, j: (i, 0)),
        ],
        out_shape=[jax.ShapeDtypeStruct((N_TOK, DN_MAIN), BF16), jax.ShapeDtypeStruct((N_TOK, LANES), F32)],
        scratch_shapes=[pltpu.VMEM((ROW_BLK, D), BF16), pltpu.VMEM((HALO_ROWS, D), BF16)],
        compiler_params=pltpu.CompilerParams(
            dimension_semantics=("arbitrary", "arbitrary"), vmem_limit_bytes=VMEM_LIMIT),
        name="deltanet_in",
    )(*(x if split_x else (x,)), mod, mod, gains, w_main, conv_w, w_ab, a_log, dt_bias)


def _bdot_nt(a, b):
    return lax.dot_general(a.astype(BF16), b.astype(BF16), (((1,), (1,)), ((), ())), preferred_element_type=F32)


def _bdot_tn(a, b):
    return lax.dot_general(a.astype(BF16), b.astype(BF16), (((0,), (0,)), ((), ())), preferred_element_type=F32)


def _bmm(a, b):
    return jnp.einsum('gmk,gkn->gmn', a.astype(BF16), b.astype(BF16), preferred_element_type=F32)


def _bmm_nt(a, b):
    return jnp.einsum('gmk,gnk->gmn', a.astype(BF16), b.astype(BF16), preferred_element_type=F32)


SOLVE_BLK = 16


def _invert_diagonal_blocks(tri, reverse):
    G, C, nb = tri.shape[0], CHUNK, CHUNK // SOLVE_BLK
    half = G // 2
    pair = jnp.concatenate([tri[:half], tri[half:]], axis=2)
    r = lax.broadcasted_iota(jnp.int32, (C, LANES), 0)
    c = lax.broadcasted_iota(jnp.int32, (C, LANES), 1) % C
    diag = jnp.where(r // SOLVE_BLK == c // SOLVE_BLK, pair, jnp.zeros_like(pair))
    kk = lax.broadcasted_iota(jnp.int32, (LANES, SOLVE_BLK * LANES), 0)
    nn = lax.broadcasted_iota(jnp.int32, (LANES, SOLVE_BLK * LANES), 1)
    pick = jnp.where((kk % SOLVE_BLK == nn // LANES) & (kk // C == (nn % LANES) // C), 1.0, 0.0).astype(BF16)
    cols = jnp.einsum('gik,kn->gin', diag, pick, preferred_element_type=F32)
    inv_d = jnp.broadcast_to(jnp.where(r == c, 1.0, 0.0), (half, C, LANES))
    steps = range(SOLVE_BLK - 1, 0, -1) if reverse else range(SOLVE_BLK - 1)
    for s in steps:
        pivot_rows = jnp.concatenate(
            [jnp.broadcast_to(inv_d[:, b * SOLVE_BLK + s:b * SOLVE_BLK + s + 1, :], (half, SOLVE_BLK, LANES))
             for b in range(nb)], axis=1)
        inv_d = inv_d - cols[:, :, s * LANES:(s + 1) * LANES] * pivot_rows
    inv_d = inv_d.astype(BF16)
    return jnp.concatenate([inv_d[:, :, :C], inv_d[:, :, C:]], axis=0)


def _block_substitution(systems):
    C, nb = CHUNK, CHUNK // SOLVE_BLK
    r = lax.broadcasted_iota(jnp.int32, (C, C), 0)
    c = lax.broadcasted_iota(jnp.int32, (C, C), 1)
    offs = [jnp.where(r // SOLVE_BLK != c // SOLVE_BLK, tri, jnp.zeros_like(tri)) for tri, _, _, _ in systems]

    def placed(block, b):
        G, _, width = block.shape
        parts = []
        if b > 0:
            parts.append(jnp.zeros((G, b * SOLVE_BLK, width), F32))
        parts.append(block)
        if b < nb - 1:
            parts.append(jnp.zeros((G, (nb - 1 - b) * SOLVE_BLK, width), F32))
        return jnp.concatenate(parts, axis=1)

    xs = [None] * len(systems)
    for step in range(nb):
        ys = []
        for n, (_, _, rhs, reverse) in enumerate(systems):
            b = nb - 1 - step if reverse else step
            rows = slice(b * SOLVE_BLK, (b + 1) * SOLVE_BLK)
            ys.append(rhs[:, rows] if xs[n] is None else rhs[:, rows] - _bmm(offs[n][:, rows], xs[n]))
        for n, (_, inv_d, _, reverse) in enumerate(systems):
            b = nb - 1 - step if reverse else step
            rows = slice(b * SOLVE_BLK, (b + 1) * SOLVE_BLK)
            xb = placed(_bmm(inv_d[:, rows], placed(ys[n], b)), b)
            xs[n] = xb if xs[n] is None else xs[n] + xb
    return xs


def _dn_chunk_terms(q, k, v, g_raw, beta, reverse):
    G, C = q.shape[0], CHUNK
    r = lax.broadcasted_iota(jnp.int32, (C, C), 0)
    c = lax.broadcasted_iota(jnp.int32, (C, C), 1)
    incl = (r <= c) if reverse else (r >= c)
    strict = (r < c) if reverse else (r > c)
    g_mat = jnp.broadcast_to(g_raw, (G, C, C))
    scanned = (r >= c) if reverse else (r <= c)
    cum_row = jnp.sum(jnp.where(scanned, g_mat, 0.0), axis=1, keepdims=True)
    cum_col = jnp.sum(jnp.where(r == c, jnp.broadcast_to(cum_row, (G, C, C)), 0.0), axis=2, keepdims=True)
    decay = jnp.exp(jnp.where(incl, cum_col - cum_row, -jnp.inf))
    kb = k * beta
    kk = _bmm_nt(kb, k) * decay
    qk = _bmm_nt(q, k) * decay
    tri = jnp.where(strict, kk, 0.0).astype(BF16)
    e_col = jnp.exp(cum_col)
    rhs = jnp.concatenate([v * beta, kb * e_col], axis=2)
    total = cum_col[:, 0:1] if reverse else cum_col[:, C - 1:C]
    k_end = k * jnp.exp(total - cum_col)
    return tri, rhs, q * e_col, k_end, qk, jnp.exp(total)


PREP_BATCH = 32


def _dn_scan_kernel(q_ref, k_ref, v_ref, z_ref, g_ref, s0_ref, on_ref, y_ref, *rest, seq_len, with_final, HG):
    if with_final:
        sfin_ref, rest = rest[0], rest[1:]
    u_scr, wq_scr, kq_scr, el_scr, o_scr, s_scr = rest
    n = seq_len // CHUNK
    G = PREP_BATCH // HG
    head0 = pl.program_id(1) * HG
    lane = lax.broadcasted_iota(jnp.int32, (G, CHUNK, LANES), 2)

    def prepare(grp, carry):
        rows = pl.ds(pl.multiple_of(grp * (G * CHUNK), G * CHUNK), G * CHUNK)

        def by_head(ref):
            x = ref[rows, :].astype(F32)
            return jnp.concatenate([x[:, hh * DK:(hh + 1) * DK].reshape(G, CHUNK, DK) for hh in range(HG)], axis=0)

        q, k, v = by_head(q_ref), by_head(k_ref), by_head(v_ref)
        gates = g_ref[rows, :].reshape(G, CHUNK, LANES)

        def gate(col):
            return jnp.concatenate(
                [jnp.sum(jnp.where(lane == col + head0 + hh, gates, 0.0), axis=2, keepdims=True) for hh in range(HG)],
                axis=0)

        terms = [_dn_chunk_terms(q, k, v, gate(d * HEADS_DN), gate((2 + d) * HEADS_DN), reverse=(d == 1))
                 for d in range(2)]
        inverses = [_invert_diagonal_blocks(terms[d][0], reverse=(d == 1)) for d in range(2)]
        sols = _block_substitution([(terms[d][0], inverses[d], terms[d][1], d == 1) for d in range(2)])
        for d in range(2):
            _, _, qe, ke, qk, el = terms[d]
            for hh in range(HG):
                part = slice(hh * G, (hh + 1) * G)
                u_scr[d, hh, rows, :] = sols[d][part, :, :DK].reshape(G * CHUNK, DK)
                wq = jnp.concatenate([sols[d][part, :, DK:], qe[part]], axis=1).astype(BF16)
                wq_scr[d, hh, pl.ds(pl.multiple_of(grp * (G * 2 * CHUNK), 2 * CHUNK), G * 2 * CHUNK), :] = (
                    wq.reshape(G * 2 * CHUNK, DK))
                kq = jnp.concatenate([qk[part], jnp.swapaxes(ke[part], 1, 2)], axis=1).astype(BF16)
                kq_scr[d, hh, pl.ds(pl.multiple_of(grp * (G * 3 * CHUNK), 3 * CHUNK), G * 3 * CHUNK), :] = (
                    kq.reshape(G * 3 * CHUNK, CHUNK))
                el_scr[d, hh, pl.ds(grp * G, G)] = jnp.broadcast_to(el[part], (G, 8, LANES))
        return carry

    lax.fori_loop(0, n // G, prepare, 0)
    s_scr[...] = s0_ref[...]

    def scan(t, carry):
        chains = [(hh, d) for hh in range(HG) for d in range(2)]
        chunk = [t if d == 0 else n - 1 - t for _, d in chains]
        states = [s_scr[d, hh] for hh, d in chains]
        from_state = []
        for (hh, d), c, s in zip(chains, chunk, states):
            lhs = wq_scr[d, hh, pl.ds(pl.multiple_of(c * (2 * CHUNK), 2 * CHUNK), 2 * CHUNK), :]
            from_state.append(jnp.dot(lhs, s.astype(BF16), preferred_element_type=F32))
        for (hh, d), c, s, ws_qs in zip(chains, chunk, states, from_state):
            r = pl.ds(pl.multiple_of(c * CHUNK, CHUNK), CHUNK)
            vb = (u_scr[d, hh, r, :] - ws_qs[:CHUNK]).astype(BF16)
            lhs = kq_scr[d, hh, pl.ds(pl.multiple_of(c * (3 * CHUNK), 3 * CHUNK), 3 * CHUNK), :]
            from_new = jnp.dot(lhs, vb, preferred_element_type=F32)
            o_scr[d, hh, r, :] = ws_qs[CHUNK:] + from_new[:CHUNK]
            s_scr[d, hh] = s * el_scr[d, hh, c][0:1, :] + from_new[CHUNK:]
        return carry

    lax.fori_loop(0, n, scan, 0)
    if with_final:
        sfin_ref[...] = s_scr[...]
    for hh in range(HG):
        cols = slice(hh * DK, (hh + 1) * DK)
        o = o_scr[0, hh] + o_scr[1, hh]
        y_ref[:, cols] = (_rms(o, on_ref[...]) * _silu(z_ref[:, cols].astype(F32))).astype(BF16)


def _dn_scan(p, gates, s0, s0_index, onorm_g, *, n_seq, seq_len, row_offset, with_final):
    blk0 = row_offset // seq_len
    n = seq_len // CHUNK
    HG = max(PREP_BATCH // n, HEADS_DN // 2)
    n_grp = HEADS_DN // HG

    def part(k):
        return pl.BlockSpec((seq_len, HG * DK), lambda b, h: (blk0 + b, k * n_grp + h))

    out_specs = [pl.BlockSpec((seq_len, HG * DK), lambda b, h: (b, h))]
    out_shape = [jax.ShapeDtypeStruct((n_seq * seq_len, D), BF16)]
    if with_final:
        out_specs.append(pl.BlockSpec((None, 2, HG, DK, DK), lambda b, h: (b, 0, h, 0, 0)))
        out_shape.append(jax.ShapeDtypeStruct((n_seq, 2, HEADS_DN, DK, DK), F32))
    n = seq_len // CHUNK
    res = pl.pallas_call(
        functools.partial(_dn_scan_kernel, seq_len=seq_len, with_final=with_final, HG=HG),
        grid=(n_seq, n_grp),
        in_specs=[
            part(0), part(1), part(2), part(3),
            pl.BlockSpec((seq_len, LANES), lambda b, h: (blk0 + b, 0)),
            pl.BlockSpec((None,) * (s0.ndim - 4) + (2, HG, DK, DK), s0_index),
            pl.BlockSpec((1, DK), lambda b, h: (0, 0)),
        ],
        out_specs=out_specs,
        out_shape=out_shape,
        scratch_shapes=[
            pltpu.VMEM((2, HG, seq_len, DK), F32),
            pltpu.VMEM((2, HG, 2 * seq_len, DK), BF16),
            pltpu.VMEM((2, HG, 3 * seq_len, CHUNK), BF16),
            pltpu.VMEM((2, HG, n, 8, LANES), F32),
            pltpu.VMEM((2, HG, seq_len, DK), F32),
            pltpu.VMEM((2, HG, DK, DK), F32),
        ],
        compiler_params=pltpu.CompilerParams(
            dimension_semantics=("arbitrary", "arbitrary"), vmem_limit_bytes=VMEM_LIMIT),
        name="deltanet_scan_%d" % seq_len,
    )(p, p, p, p, gates, s0, onorm_g)
    return res


def _rope_tile(x, cos, sin):
    lane = lax.broadcasted_iota(jnp.int32, x.shape, 1)
    width = x.shape[1]
    partner = jnp.where((lane & 31) < 16, pltpu.roll(x, width - 16, 1), pltpu.roll(x, 16, 1))
    return x * cos + partner * sin


def _at_in_kernel(x_ref, sh_ref, sc_ref, gpre_ref, w_ref, cos_ref, sin_ref, q_ref, kv_ref, h_scr):
    i = pl.program_id(0)
    h_scr[...] = _pre(x_ref[...], gpre_ref[...], sh_ref[...], sc_ref[...]).astype(BF16)
    stages = [(s, t) for s in range(N_SEG) for t in _sub_tiles(AT_Q + 2 * AT_KVW)]

    def project(stage):
        s, (c0, cw) = stage
        return jnp.dot(h_scr[s * SEG:(s + 1) * SEG, :], w_ref[:, c0:c0 + cw], preferred_element_type=F32)

    def consume(stage, u, rope):
        s, (c0, cw) = stage
        rows = slice(s * SEG, (s + 1) * SEG)
        if rope and c0 < AT_Q + AT_KVW:
            u = jnp.concatenate([_rope_tile(u[:, t * LANES:(t + 1) * LANES], cos_ref[rows, :], sin_ref[rows, :])
                                 for t in range(cw // LANES)], axis=1)
        if c0 < AT_Q:
            q_ref[rows, c0:c0 + cw] = (u * Q_SCALE).astype(BF16)
        else:
            kv_ref[rows, c0 - AT_Q:c0 - AT_Q + cw] = u

    @pl.when(i >= PROMPT_BLKS)
    def _():
        _pipelined(stages, project, functools.partial(consume, rope=True))

    @pl.when(i < PROMPT_BLKS)
    def _():
        _pipelined(stages, project, functools.partial(consume, rope=False))


def _at_in_layer(x, mod, gains, layer, w_qkv, cos, sin):
    width = AT_Q + 2 * AT_KVW
    row = pl.BlockSpec((ROW_BLK, D), lambda i: (i, 0))
    tab = pl.BlockSpec((ROW_BLK, LANES), lambda i: (0, 0))
    return pl.pallas_call(
        _at_in_kernel,
        grid=(N_BLKS,),
        in_specs=[row, _mod_spec(layer, 0), _mod_spec(layer, 1), _gain_spec(layer, 0),
                  pl.BlockSpec((D, width), lambda i: (0, 0)), tab, tab],
        out_specs=[pl.BlockSpec((ROW_BLK, AT_Q), lambda i: (i, 0)),
                   pl.BlockSpec((ROW_BLK, 2 * AT_KVW), lambda i: (i, 0))],
        out_shape=[jax.ShapeDtypeStruct((N_TOK, AT_Q), BF16), jax.ShapeDtypeStruct((N_TOK, 2 * AT_KVW), F32)],
        scratch_shapes=[pltpu.VMEM((ROW_BLK, D), BF16)],
        compiler_params=pltpu.CompilerParams(
            dimension_semantics=("arbitrary",), vmem_limit_bytes=VMEM_LIMIT),
        name="attention_in",
    )(x, mod, mod, gains, w_qkv, cos, sin)


def _softmax_pv(scores, values, sink):
    m = sink
    for s in scores:
        m = jnp.maximum(m, jnp.max(s, axis=0, keepdims=True))
    denom = jnp.exp2(sink - m)
    acc = None
    for s, v in zip(scores, values):
        p = jnp.exp2(s - m)
        denom = denom + jnp.sum(p, axis=0, keepdims=True)
        pv = _bdot_tn(v, p)
        acc = pv if acc is None else acc + pv
    return acc / denom


def _attend_groups(q_ref, scores_of, values_of, sink_ref):
    rows = q_ref.shape[0]

    def stacked_q(kvh):
        return jnp.concatenate([q_ref[:, (kvh * AT_G + g) * HD:(kvh * AT_G + g + 1) * HD] for g in range(AT_G)], axis=0)

    def stacked_sink(kvh):
        return jnp.concatenate([jnp.broadcast_to(sink_ref[0:1, kvh * AT_G + g:kvh * AT_G + g + 1], (1, rows))
                                for g in range(AT_G)], axis=1)

    outs = []
    pending = scores_of(stacked_q(0), 0)
    for kvh in range(AT_KV):
        upcoming = scores_of(stacked_q(kvh + 1), kvh + 1) if kvh + 1 < AT_KV else None
        o = _softmax_pv(pending, values_of(kvh), stacked_sink(kvh))
        outs.extend(o[:, g * rows:(g + 1) * rows] for g in range(AT_G))
        pending = upcoming
    return jnp.concatenate(outs, axis=0).T.astype(BF16)


def _ctx_attn_kernel(q_ref, kv_ref, sink_ref, y_ref):
    keys = [kv_ref[:, kvh * HD:(kvh + 1) * HD].astype(BF16) for kvh in range(AT_KV)]
    vals = [kv_ref[:, AT_KVW + kvh * HD:AT_KVW + (kvh + 1) * HD].astype(BF16) for kvh in range(AT_KV)]
    y_ref[...] = _attend_groups(q_ref, lambda q, kvh: [_bdot_nt(keys[kvh], q)], lambda kvh: [vals[kvh]], sink_ref)


def _ctx_attention(q, kv, sink):
    nq = L_PROMPT // QBLK
    return pl.pallas_call(
        _ctx_attn_kernel,
        grid=(N_PROMPT, nq),
        in_specs=[
            pl.BlockSpec((QBLK, AT_Q), lambda b, t: (b * nq + t, 0)),
            pl.BlockSpec((L_PROMPT, 2 * AT_KVW), lambda b, t: (b, 0)),
            pl.BlockSpec((1, LANES), lambda b, t: (0, 0)),
        ],
        out_specs=pl.BlockSpec((QBLK, AT_Q), lambda b, t: (b * nq + t, 0)),
        out_shape=jax.ShapeDtypeStruct((TOK_PROMPT, AT_Q), BF16),
        compiler_params=pltpu.CompilerParams(dimension_semantics=("arbitrary", "arbitrary")),
        name="context_attention",
    )(q, kv, sink)


def _lat_attn_kernel(q_ref, kv_ref, ck_ref, cv_ref, sink_ref, y_ref):
    t = pl.program_id(1)
    k0 = pl.multiple_of(jnp.clip(t * QBLK - WINDOW, 0, L_LATENT - BAND), QBLK)
    kpos = k0 + lax.broadcasted_iota(jnp.int32, (BAND, QBLK), 0)
    qpos = t * QBLK + lax.broadcasted_iota(jnp.int32, (BAND, QBLK), 1)
    valid = jnp.concatenate([jnp.abs(qpos - kpos) <= WINDOW] * AT_G, axis=1)
    band = pl.ds(k0, BAND)
    k_loc = [kv_ref[band, kvh * HD:(kvh + 1) * HD].astype(BF16) for kvh in range(AT_KV)]
    v_loc = [kv_ref[band, AT_KVW + kvh * HD:AT_KVW + (kvh + 1) * HD].astype(BF16) for kvh in range(AT_KV)]
    k_ctx = [ck_ref[:, kvh * HD:(kvh + 1) * HD].astype(BF16) for kvh in range(AT_KV)]
    v_ctx = [cv_ref[:, kvh * HD:(kvh + 1) * HD].astype(BF16) for kvh in range(AT_KV)]

    def scores_of(q, kvh):
        return [jnp.where(valid, _bdot_nt(k_loc[kvh], q), -jnp.inf), _bdot_nt(k_ctx[kvh], q)]

    y_ref[...] = _attend_groups(q_ref, scores_of, lambda kvh: [v_loc[kvh], v_ctx[kvh]], sink_ref)


def _lat_attention(q, kv, cache_k, cache_v, sink, attn_layer):
    nq = L_LATENT // QBLK
    q0 = TOK_PROMPT // QBLK
    s0 = TOK_PROMPT // L_LATENT
    cache = pl.BlockSpec((None, None, PAST, AT_KVW), lambda b, t: (b, attn_layer, 0, 0))
    return pl.pallas_call(
        _lat_attn_kernel,
        grid=(N_LATENT, nq),
        in_specs=[
            pl.BlockSpec((QBLK, AT_Q), lambda b, t: (q0 + b * nq + t, 0)),
            pl.BlockSpec((L_LATENT, 2 * AT_KVW), lambda b, t: (s0 + b, 0)),
            cache, cache,
            pl.BlockSpec((1, LANES), lambda b, t: (0, 0)),
        ],
        out_specs=pl.BlockSpec((QBLK, AT_Q), lambda b, t: (b * nq + t, 0)),
        out_shape=jax.ShapeDtypeStruct((N_LATENT * L_LATENT, AT_Q), BF16),
        compiler_params=pltpu.CompilerParams(dimension_semantics=("arbitrary", "arbitrary")),
        name="latent_attention",
    )(q, kv, cache_k, cache_v, sink)


def _rope_tables():
    pos = jnp.arange(L_LATENT)
    half = HD // 2
    inv = jnp.power(ROPE_BASE, -jnp.arange(0, half, 2, dtype=F32) / half)
    ang_row = (pos // GRID_W).astype(F32)[:, None] * inv
    ang_col = (pos % GRID_W).astype(F32)[:, None] * inv
    ang = jnp.concatenate([ang_row, ang_row, ang_col, ang_col], axis=1)
    sign = jnp.tile(jnp.concatenate([-jnp.ones((16,), F32), jnp.ones((16,), F32)]), 2)
    cos = jnp.tile(jnp.cos(ang), (1, 2))
    sin = jnp.tile(jnp.sin(ang) * sign, (1, 2))
    return cos, sin


def _pad_lanes(v):
    v = v.reshape(1, -1).astype(F32)
    return jnp.pad(v, ((0, 0), (0, LANES - v.shape[1])))


def kernel(x_prompt, x_sample, state_delta, cache_k, cache_v, c, c_ctx, w_ada, b_ada, norm_g, dn_w_in, dn_conv_w, dn_a_log, dn_dt_bias, dn_onorm_g, dn_w_out, at_w_qkv, at_sink, at_w_o, ffn_w_up, ffn_conv_w, ffn_conv_b, ffn_w_down):
    x = (x_prompt.reshape(TOK_PROMPT, D), x_sample.reshape(N_TOK - TOK_PROMPT, D))
    cvecs = jnp.concatenate([c_ctx[None, :], c, jnp.zeros((MOD_ROWS - 1 - N_LATENT, D), F32)], axis=0)
    mod = _modulation(cvecs, w_ada, b_ada).reshape(DEPTH, MOD_ROWS, 1, 6 * D)
    gains = norm_g.reshape(DEPTH, 4, 1, D)
    cos, sin = _rope_tables()
    ck = cache_k.reshape(N_LATENT, -1, PAST, AT_KVW)
    cv = cache_v.reshape(N_LATENT, -1, PAST, AT_KVW)
    zero_state = jnp.zeros((2, HEADS_DN, DK, DK), F32)

    states, new_k, new_v = [], [], []
    for layer in range(DEPTH):
        j = layer // 2
        if layer % 2 == 0:
            w_in = dn_w_in[j]
            w_main = w_in[:, :DN_MAIN].astype(BF16)
            w_ab = jnp.pad(w_in[:, DN_MAIN:], ((0, 0), (0, LANES - 4 * HEADS_DN))).astype(BF16)
            conv_w = jnp.pad(dn_conv_w[j], ((0, 0), (0, D)))
            p, gates = _dn_in_layer(x, mod, gains, layer, w_main, conv_w, w_ab,
                                    _pad_lanes(dn_a_log[j]), _pad_lanes(dn_dt_bias[j]))
            onorm = dn_onorm_g[j].reshape(1, DK)
            y_p, s_fin = _dn_scan(p, gates, zero_state, lambda b, h: (0, h, 0, 0), onorm,
                                  n_seq=N_PROMPT, seq_len=L_PROMPT, row_offset=0, with_final=True)
            (y_s,) = _dn_scan(p, gates, state_delta, lambda b, h, j=j: (b, j, 0, h, 0, 0), onorm,
                              n_seq=N_LATENT, seq_len=L_LATENT, row_offset=TOK_PROMPT, with_final=False)
            states.append(s_fin)
            w_o = dn_w_out[j]
        else:
            q, kv = _at_in_layer(x, mod, gains, layer, at_w_qkv[j].astype(BF16), cos, sin)
            sink = _pad_lanes(at_sink[j]) * LOG2E
            y_p = _ctx_attention(q, kv, sink)
            y_s = _lat_attention(q, kv, ck, cv, sink, j)
            new_k.append(kv[:TOK_PROMPT, :AT_KVW].reshape(N_PROMPT, L_PROMPT, AT_KV, HD))
            new_v.append(kv[:TOK_PROMPT, AT_KVW:].reshape(N_PROMPT, L_PROMPT, AT_KV, HD))
            w_o = at_w_o[j]
        x = _out_layer(y_p, y_s, w_o.astype(BF16), x, mod, gains, layer)
        x = _ffn_layer(x, mod, gains, layer, ffn_w_up[layer].astype(BF16), ffn_conv_w[layer],
                       ffn_conv_b[layer].reshape(1, 2 * FFN), ffn_w_down[layer].astype(BF16),
                       split_out=(layer == DEPTH - 1))

    y_prompt = x[0].reshape(N_PROMPT, L_PROMPT, D)
    y_sample = x[1].reshape(N_LATENT, L_LATENT, D)
    return (y_prompt, y_sample, jnp.stack(states, axis=1), jnp.stack(new_k, axis=1), jnp.stack(new_v, axis=1))
```

```python
import functools

import jax
import jax.numpy as jnp
from jax import lax
from jax.experimental import pallas as pl
from jax.experimental.pallas import tpu as pltpu

F32 = jnp.float32
BF16 = jnp.bfloat16

D = 1024
N_PROMPT, L_PROMPT = 16, 256
N_LATENT, L_LATENT = 8, 1024
PAST = 512
GRID_W = 64
DEPTH = 4
HEADS_DN, DK = 8, 128
CHUNK = 64
DN_MAIN = 4 * D
AT_HEADS, AT_KV, AT_G, HD = 16, 4, 4, 64
AT_Q = AT_HEADS * HD
AT_KVW = AT_KV * HD
WINDOW = 128
QBLK = 128
BAND = QBLK + 2 * WINDOW
FFN = 2816
EPS = 1e-6
ROPE_BASE = 10000.0
LOG2E = 1.4426950408889634
Q_SCALE = HD ** -0.5 * LOG2E

ROW_BLK = 1024
TOK_PROMPT = N_PROMPT * L_PROMPT
N_TOK = TOK_PROMPT + N_LATENT * L_LATENT
PROMPT_BLKS = TOK_PROMPT // ROW_BLK
N_BLKS = N_TOK // ROW_BLK
MOD_ROWS = 16
LANES = 128
FFN_TN = 2816
MXU_N = 256
DN_TN = 1024
VMEM_LIMIT = 56 * 1024 * 1024


def _mod_row(i):
    return jnp.where(i < PROMPT_BLKS, 0, i - (PROMPT_BLKS - 1))


def _mod_spec(layer, chunk):
    return pl.BlockSpec((None, None, 1, D), lambda i, *_: (layer, _mod_row(i), 0, chunk))


def _gain_spec(layer, k):
    return pl.BlockSpec((None, None, 1, D), lambda i, *_: (layer, k, 0, 0))


def _split_specs(width):
    return [pl.BlockSpec((ROW_BLK, width), lambda i, *_: (jnp.minimum(i, PROMPT_BLKS - 1), 0)),
            pl.BlockSpec((ROW_BLK, width), lambda i, *_: (jnp.maximum(i - PROMPT_BLKS, 0), 0))]


def _pick(i, prompt_ref, latent_ref):
    return jnp.where(i < PROMPT_BLKS, prompt_ref[...], latent_ref[...])


def _rms(x, g):
    return x * lax.rsqrt(jnp.mean(x * x, axis=-1, keepdims=True) + EPS) * g


def _pre(x, g, shift, scale):
    return _rms(x, g) * (1.0 + scale) + shift


def _silu(x):
    half = 0.5 * x
    return half + half * jnp.tanh(half)


SEG = L_PROMPT
N_SEG = ROW_BLK // SEG
HALO_ROWS = 16


def _halo_rows(h, i):
    inside = jnp.where(i < PROMPT_BLKS, 0.0, 1.0)
    picks = []
    for s in range(1, N_SEG):
        picks += [h[s * SEG - 1:s * SEG], h[s * SEG:s * SEG + 1]]
    picks.append(jnp.zeros((HALO_ROWS - len(picks), h.shape[1]), F32))
    return (jnp.concatenate(picks, axis=0) * inside).astype(BF16)


def _conv3(u, w, halo, s):
    sub = lax.broadcasted_iota(jnp.int32, (8, 1), 0)
    zero_row = jnp.zeros((1, u.shape[1]), F32)
    before = halo[2 * s - 2:2 * s - 1] if s > 0 else zero_row
    after = halo[2 * s + 1:2 * s + 2] if s < N_SEG - 1 else zero_row
    down = pltpu.roll(u, 1, 0)
    up = pltpu.roll(u, SEG - 1, 0)
    prev = jnp.concatenate([jnp.where(sub == 0, before, down[:8]), down[8:]], axis=0)
    nxt = jnp.concatenate([up[:SEG - 8], jnp.where(sub == 7, after, up[SEG - 8:])], axis=0)
    return prev * w[0:1] + u * w[1:2] + nxt * w[2:3]


def _pipelined(stages, project, consume, ahead=1):
    pending = [project(st) for st in stages[:ahead]]
    for n, stage in enumerate(stages):
        if n + ahead < len(stages):
            pending.append(project(stages[n + ahead]))
        consume(stage, pending.pop(0))


def _sub_tiles(width):
    tiles, off = [], 0
    while off < width:
        size = min(MXU_N, width - off)
        tiles.append((off, size))
        off += size
    return tiles


def _mod_kernel(c_ref, w_ref, b_ref, o_ref):
    s = _silu(c_ref[...]).astype(BF16)
    o_ref[...] = jnp.dot(s, w_ref[...].astype(BF16), preferred_element_type=F32) + b_ref[...]


def _modulation(cvecs, w_ada, b_ada):
    tn = 1536
    return pl.pallas_call(
        _mod_kernel,
        grid=(DEPTH, 6 * D // tn),
        in_specs=[
            pl.BlockSpec((MOD_ROWS, D), lambda l, j: (0, 0)),
            pl.BlockSpec((None, D, tn), lambda l, j: (l, 0, j)),
            pl.BlockSpec((None, 1, tn), lambda l, j: (l, 0, j)),
        ],
        out_specs=pl.BlockSpec((None, MOD_ROWS, tn), lambda l, j: (l, 0, j)),
        out_shape=jax.ShapeDtypeStruct((DEPTH, MOD_ROWS, 6 * D), F32),
        compiler_params=pltpu.CompilerParams(
            dimension_semantics=("arbitrary", "arbitrary"), vmem_limit_bytes=VMEM_LIMIT),
        name="modulation",
    )(cvecs, w_ada, b_ada.reshape(DEPTH, 1, 6 * D))


def _ffn_kernel(x_ref, sh_ref, sc_ref, gt_ref, gpre_ref, gpost_ref, wa_ref, wb_ref, cwa_ref, cwb_ref,
                cba_ref, cbb_ref, wd_ref, *rest, split_out):
    outs, (h_scr, halo_scr, acc_scr) = rest[:-3], rest[-3:]
    i, j = pl.program_id(0), pl.program_id(1)

    @pl.when(j == 0)
    def _():
        h = _pre(x_ref[...], gpre_ref[...], sh_ref[...], sc_ref[...])
        h_scr[...] = h.astype(BF16)
        halo_scr[...] = _halo_rows(h, i)
        acc_scr[...] = jnp.zeros_like(acc_scr)

    def project(stage):
        s, (c0, cw) = stage
        h = h_scr[s * SEG:(s + 1) * SEG, :]
        cols = slice(c0, c0 + cw)
        return tuple(jnp.dot(lhs, w[:, cols], preferred_element_type=F32)
                     for w in (wa_ref, wb_ref) for lhs in (h, halo_scr[...]))

    def consume(stage, up):
        s, (c0, cw) = stage
        cols = slice(c0, c0 + cw)
        a = _conv3(up[0], cwa_ref[:, cols], up[1], s) + cba_ref[:, cols]
        b = _conv3(up[2], cwb_ref[:, cols], up[3], s) + cbb_ref[:, cols]
        act = (_silu(a) * b).astype(BF16)
        acc_scr[s * SEG:(s + 1) * SEG, :] += jnp.dot(act, wd_ref[cols, :], preferred_element_type=F32)

    _pipelined([(s, t) for s in range(N_SEG) for t in _sub_tiles(FFN_TN)], project, consume)

    last = j == pl.num_programs(1) - 1
    if split_out:
        @pl.when(jnp.logical_and(last, i < PROMPT_BLKS))
        def _():
            outs[0][...] = x_ref[...] + gt_ref[...] * _rms(acc_scr[...], gpost_ref[...])

        @pl.when(jnp.logical_and(last, i >= PROMPT_BLKS))
        def _():
            outs[1][...] = x_ref[...] + gt_ref[...] * _rms(acc_scr[...], gpost_ref[...])
    else:
        @pl.when(last)
        def _():
            outs[0][...] = x_ref[...] + gt_ref[...] * _rms(acc_scr[...], gpost_ref[...])


def _ffn_layer(x, mod, gains, layer, w_up, conv_w, conv_b, w_down, split_out=False):
    nj = FFN // FFN_TN
    row = pl.BlockSpec((ROW_BLK, D), lambda i, j: (i, 0))
    if split_out:
        out_specs = _split_specs(D)
        out_shape = [jax.ShapeDtypeStruct((TOK_PROMPT, D), F32), jax.ShapeDtypeStruct((N_TOK - TOK_PROMPT, D), F32)]
    else:
        out_specs, out_shape = [row], [jax.ShapeDtypeStruct((N_TOK, D), F32)]
    res = pl.pallas_call(
        functools.partial(_ffn_kernel, split_out=split_out),
        grid=(N_BLKS, nj),
        in_specs=[
            row, _mod_spec(layer, 3), _mod_spec(layer, 4), _mod_spec(layer, 5),
            _gain_spec(layer, 2), _gain_spec(layer, 3),
            pl.BlockSpec((D, FFN_TN), lambda i, j: (0, j), pipeline_mode=pl.Buffered(1)),
            pl.BlockSpec((D, FFN_TN), lambda i, j: (0, nj + j), pipeline_mode=pl.Buffered(1)),
            pl.BlockSpec((3, FFN_TN), lambda i, j: (0, j)),
            pl.BlockSpec((3, FFN_TN), lambda i, j: (0, nj + j)),
            pl.BlockSpec((1, FFN_TN), lambda i, j: (0, j)),
            pl.BlockSpec((1, FFN_TN), lambda i, j: (0, nj + j)),
            pl.BlockSpec((FFN_TN, D), lambda i, j: (j, 0), pipeline_mode=pl.Buffered(1)),
        ],
        out_specs=out_specs,
        out_shape=out_shape,
        scratch_shapes=[pltpu.VMEM((ROW_BLK, D), BF16), pltpu.VMEM((HALO_ROWS, D), BF16),
                        pltpu.VMEM((ROW_BLK, D), F32)],
        compiler_params=pltpu.CompilerParams(
            dimension_semantics=("arbitrary", "arbitrary"), vmem_limit_bytes=VMEM_LIMIT),
        name="conv_ffn",
    )(x, mod, mod, mod, gains, gains, w_up, w_up, conv_w, conv_w, conv_b, conv_b, w_down)
    return tuple(res) if split_out else res[0]


def _out_kernel(yp_ref, ys_ref, w_ref, *rest, split_x):
    i = pl.program_id(0)
    x_refs, (gt_ref, g_ref, o_ref) = rest[:-3], rest[-3:]

    def run(y_ref, x_ref):
        def project(s):
            return jnp.dot(y_ref[s * SEG:(s + 1) * SEG, :], w_ref[...], preferred_element_type=F32)

        def consume(s, y):
            rows = slice(s * SEG, (s + 1) * SEG)
            o_ref[rows, :] = x_ref[rows, :] + gt_ref[...] * _rms(y, g_ref[...])

        _pipelined(list(range(N_SEG)), project, consume)

    @pl.when(i < PROMPT_BLKS)
    def _():
        run(yp_ref, x_refs[0])

    @pl.when(i >= PROMPT_BLKS)
    def _():
        run(ys_ref, x_refs[-1])


def _out_layer(y_prompt, y_latent, w, x, mod, gains, layer):
    split_x = isinstance(x, tuple)
    row = pl.BlockSpec((ROW_BLK, D), lambda i: (i, 0))
    return pl.pallas_call(
        functools.partial(_out_kernel, split_x=split_x),
        grid=(N_BLKS,),
        in_specs=_split_specs(D) + [pl.BlockSpec((D, D), lambda i: (0, 0))]
        + (_split_specs(D) if split_x else [row]) + [_mod_spec(layer, 2), _gain_spec(layer, 1)],
        out_specs=row,
        out_shape=jax.ShapeDtypeStruct((N_TOK, D), F32),
        compiler_params=pltpu.CompilerParams(
            dimension_semantics=("arbitrary",), vmem_limit_bytes=VMEM_LIMIT),
        name="mixer_out",
    )(y_prompt, y_latent, w, *(x if split_x else (x,)), mod, gains)


def _dn_in_kernel(*refs, split_x):
    x_refs, refs = (refs[:2], refs[2:]) if split_x else (refs[:1], refs[1:])
    sh_ref, sc_ref, gpre_ref, w_ref, cw_ref, wab_ref, alog_ref, dtb_ref, p_ref, g_ref, h_scr, halo_scr = refs
    i, j = pl.program_id(0), pl.program_id(1)

    @pl.when(j == 0)
    def _():
        x = _pick(i, *x_refs) if split_x else x_refs[0][...]
        hf = _pre(x, gpre_ref[...], sh_ref[...], sc_ref[...])
        h = hf.astype(BF16)
        h_scr[...] = h
        halo_scr[...] = _halo_rows(hf, i)
        ab = jnp.dot(h, wab_ref[...], preferred_element_type=F32)
        t = ab + dtb_ref[...]
        softplus = jnp.maximum(t, 0.0) + jnp.log(1.0 + jnp.exp(-jnp.abs(t)))
        lane = lax.broadcasted_iota(jnp.int32, ab.shape, 1)
        g_ref[...] = jnp.where(lane < 2 * HEADS_DN, -jnp.exp(alog_ref[...]) * softplus, jax.nn.sigmoid(ab))

    stages = [(s, t) for s in range(N_SEG) for t in _sub_tiles(DN_TN)]

    def project(stage, with_halo=True):
        s, (c0, cw) = stage
        w = w_ref[:, c0:c0 + cw]
        u = jnp.dot(h_scr[s * SEG:(s + 1) * SEG, :], w, preferred_element_type=F32)
        return (u, jnp.dot(halo_scr[...], w, preferred_element_type=F32)) if with_halo else u

    def activated(stage, up):
        s, (c0, cw) = stage
        return _silu(_conv3(up[0], cw_ref[:, c0:c0 + cw], up[1], s))

    @pl.when(j < 2)
    def _():
        qscale = jnp.where(j == 0, DK ** -0.5, 1.0)

        def consume(stage, up):
            s, (c0, cw) = stage
            act = activated(stage, up)
            for t in range(cw // DK):
                xs = act[:, t * DK:(t + 1) * DK]
                inv = lax.rsqrt(jnp.sum(xs * xs, axis=-1, keepdims=True) + EPS) * qscale
                p_ref[s * SEG:(s + 1) * SEG, c0 + t * DK:c0 + (t + 1) * DK] = (xs * inv).astype(BF16)

        _pipelined(stages, project, consume)

    @pl.when(j == 2)
    def _():
        def consume(stage, up):
            s, (c0, cw) = stage
            p_ref[s * SEG:(s + 1) * SEG, c0:c0 + cw] = activated(stage, up).astype(BF16)

        _pipelined(stages, project, consume)

    @pl.when(j == 3)
    def _():
        def consume(stage, u):
            s, (c0, cw) = stage
            p_ref[s * SEG:(s + 1) * SEG, c0:c0 + cw] = u.astype(BF16)

        _pipelined(stages, functools.partial(project, with_halo=False), consume)


def _dn_in_layer(x, mod, gains, layer, w_main, conv_w, w_ab, a_log, dt_bias):
    split_x = isinstance(x, tuple)
    row = pl.BlockSpec((ROW_BLK, D), lambda i, j: (i, 0))
    vec = pl.BlockSpec((1, LANES), lambda i, j: (0, 0))
    return pl.pallas_call(
        functools.partial(_dn_in_kernel, split_x=split_x),
        grid=(N_BLKS, DN_MAIN // DN_TN),
        in_specs=(_split_specs(D) if split_x else [row]) + [
            _mod_spec(layer, 0), _mod_spec(layer, 1), _gain_spec(layer, 0),
            pl.BlockSpec((D, DN_TN), lambda i, j: (0, j)),
            pl.BlockSpec((3, DN_TN), lambda i, j: (0, j)),
            pl.BlockSpec((D, LANES), lambda i, j: (0, 0)),
            vec, vec,
        ],
        out_specs=[
            pl.BlockSpec((ROW_BLK, DN_TN), lambda i, j: (i, j)),
            pl.BlockSpec((ROW_BLK, LANES), lambda i, j: (i, 0)),
        ],
        out_shape=[jax.ShapeDtypeStruct((N_TOK, DN_MAIN), BF16), jax.ShapeDtypeStruct((N_TOK, LANES), F32)],
        scratch_shapes=[pltpu.VMEM((ROW_BLK, D), BF16), pltpu.VMEM((HALO_ROWS, D), BF16)],
        compiler_params=pltpu.CompilerParams(
            dimension_semantics=("arbitrary", "arbitrary"), vmem_limit_bytes=VMEM_LIMIT),
        name="deltanet_in",
    )(*(x if split_x else (x,)), mod, mod, gains, w_main, conv_w, w_ab, a_log, dt_bias)


def _bdot_nt(a, b):
    return lax.dot_general(a.astype(BF16), b.astype(BF16), (((1,), (1,)), ((), ())), preferred_element_type=F32)


def _bdot_tn(a, b):
    return lax.dot_general(a.astype(BF16), b.astype(BF16), (((0,), (0,)), ((), ())), preferred_element_type=F32)


def _bmm(a, b):
    return jnp.einsum('gmk,gkn->gmn', a.astype(BF16), b.astype(BF16), preferred_element_type=F32)


def _bmm_nt(a, b):
    return jnp.einsum('gmk,gnk->gmn', a.astype(BF16), b.astype(BF16), preferred_element_type=F32)


SOLVE_BLK = 16


def _invert_diagonal_blocks(tri, reverse):
    G, C, nb = tri.shape[0], CHUNK, CHUNK // SOLVE_BLK
    half = G // 2
    pair = jnp.concatenate([tri[:half], tri[half:]], axis=2)
    r = lax.broadcasted_iota(jnp.int32, (C, LANES), 0)
    c = lax.broadcasted_iota(jnp.int32, (C, LANES), 1) % C
    diag = jnp.where(r // SOLVE_BLK == c // SOLVE_BLK, pair, jnp.zeros_like(pair))
    kk = lax.broadcasted_iota(jnp.int32, (LANES, SOLVE_BLK * LANES), 0)
    nn = lax.broadcasted_iota(jnp.int32, (LANES, SOLVE_BLK * LANES), 1)
    pick = jnp.where((kk % SOLVE_BLK == nn // LANES) & (kk // C == (nn % LANES) // C), 1.0, 0.0).astype(BF16)
    cols = jnp.einsum('gik,kn->gin', diag, pick, preferred_element_type=F32)
    inv_d = jnp.broadcast_to(jnp.where(r == c, 1.0, 0.0), (half, C, LANES))
    steps = range(SOLVE_BLK - 1, 0, -1) if reverse else range(SOLVE_BLK - 1)
    for s in steps:
        pivot_rows = jnp.concatenate(
            [jnp.broadcast_to(inv_d[:, b * SOLVE_BLK + s:b * SOLVE_BLK + s + 1, :], (half, SOLVE_BLK, LANES))
             for b in range(nb)], axis=1)
        inv_d = inv_d - cols[:, :, s * LANES:(s + 1) * LANES] * pivot_rows
    inv_d = inv_d.astype(BF16)
    return jnp.concatenate([inv_d[:, :, :C], inv_d[:, :, C:]], axis=0)


def _block_substitution(systems):
    C, nb = CHUNK, CHUNK // SOLVE_BLK
    r = lax.broadcasted_iota(jnp.int32, (C, C), 0)
    c = lax.broadcasted_iota(jnp.int32, (C, C), 1)
    offs = [jnp.where(r // SOLVE_BLK != c // SOLVE_BLK, tri, jnp.zeros_like(tri)) for tri, _, _, _ in systems]

    def placed(block, b):
        G, _, width = block.shape
        parts = []
        if b > 0:
            parts.append(jnp.zeros((G, b * SOLVE_BLK, width), F32))
        parts.append(block)
        if b < nb - 1:
            parts.append(jnp.zeros((G, (nb - 1 - b) * SOLVE_BLK, width), F32))
        return jnp.concatenate(parts, axis=1)

    xs = [None] * len(systems)
    for step in range(nb):
        ys = []
        for n, (_, _, rhs, reverse) in enumerate(systems):
            b = nb - 1 - step if reverse else step
            rows = slice(b * SOLVE_BLK, (b + 1) * SOLVE_BLK)
            ys.append(rhs[:, rows] if xs[n] is None else rhs[:, rows] - _bmm(offs[n][:, rows], xs[n]))
        for n, (_, inv_d, _, reverse) in enumerate(systems):
            b = nb - 1 - step if reverse else step
            rows = slice(b * SOLVE_BLK, (b + 1) * SOLVE_BLK)
            xb = placed(_bmm(inv_d[:, rows], placed(ys[n], b)), b)
            xs[n] = xb if xs[n] is None else xs[n] + xb
    return xs


def _dn_chunk_terms(q, k, v, g_raw, beta, reverse):
    G, C = q.shape[0], CHUNK
    r = lax.broadcasted_iota(jnp.int32, (C, C), 0)
    c = lax.broadcasted_iota(jnp.int32, (C, C), 1)
    incl = (r <= c) if reverse else (r >= c)
    strict = (r < c) if reverse else (r > c)
    g_mat = jnp.broadcast_to(g_raw, (G, C, C))
    scanned = (r >= c) if reverse else (r <= c)
    cum_row = jnp.sum(jnp.where(scanned, g_mat, 0.0), axis=1, keepdims=True)
    cum_col = jnp.sum(jnp.where(r == c, jnp.broadcast_to(cum_row, (G, C, C)), 0.0), axis=2, keepdims=True)
    decay = jnp.exp(jnp.where(incl, cum_col - cum_row, -jnp.inf))
    kb = k * beta
    kk = _bmm_nt(kb, k) * decay
    qk = _bmm_nt(q, k) * decay
    tri = jnp.where(strict, kk, 0.0).astype(BF16)
    e_col = jnp.exp(cum_col)
    rhs = jnp.concatenate([v * beta, kb * e_col], axis=2)
    total = cum_col[:, 0:1] if reverse else cum_col[:, C - 1:C]
    k_end = k * jnp.exp(total - cum_col)
    return tri, rhs, q * e_col, k_end, qk, jnp.exp(total)


PREP_BATCH = 32


def _dn_scan_kernel(q_ref, k_ref, v_ref, z_ref, g_ref, s0_ref, on_ref, y_ref, *rest, seq_len, with_final, HG):
    if with_final:
        sfin_ref, rest = rest[0], rest[1:]
    u_scr, wq_scr, kq_scr, el_scr, o_scr, s_scr = rest
    n = seq_len // CHUNK
    G = PREP_BATCH // HG
    head0 = pl.program_id(1) * HG
    lane = lax.broadcasted_iota(jnp.int32, (G, CHUNK, LANES), 2)

    def prepare(grp, carry):
        rows = pl.ds(pl.multiple_of(grp * (G * CHUNK), G * CHUNK), G * CHUNK)

        def by_head(ref):
            x = ref[rows, :].astype(F32)
            return jnp.concatenate([x[:, hh * DK:(hh + 1) * DK].reshape(G, CHUNK, DK) for hh in range(HG)], axis=0)

        q, k, v = by_head(q_ref), by_head(k_ref), by_head(v_ref)
        gates = g_ref[rows, :].reshape(G, CHUNK, LANES)

        def gate(col):
            return jnp.concatenate(
                [jnp.sum(jnp.where(lane == col + head0 + hh, gates, 0.0), axis=2, keepdims=True) for hh in range(HG)],
                axis=0)

        terms = [_dn_chunk_terms(q, k, v, gate(d * HEADS_DN), gate((2 + d) * HEADS_DN), reverse=(d == 1))
                 for d in range(2)]
        inverses = [_invert_diagonal_blocks(terms[d][0], reverse=(d == 1)) for d in range(2)]
        sols = _block_substitution([(terms[d][0], inverses[d], terms[d][1], d == 1) for d in range(2)])
        for d in range(2):
            _, _, qe, ke, qk, el = terms[d]
            for hh in range(HG):
                part = slice(hh * G, (hh + 1) * G)
                u_scr[d, hh, rows, :] = sols[d][part, :, :DK].reshape(G * CHUNK, DK)
                wq = jnp.concatenate([sols[d][part, :, DK:], qe[part]], axis=1).astype(BF16)
                wq_scr[d, hh, pl.ds(pl.multiple_of(grp * (G * 2 * CHUNK), 2 * CHUNK), G * 2 * CHUNK), :] = (
                    wq.reshape(G * 2 * CHUNK, DK))
                kq = jnp.concatenate([qk[part], jnp.swapaxes(ke[part], 1, 2)], axis=1).astype(BF16)
                kq_scr[d, hh, pl.ds(pl.multiple_of(grp * (G * 3 * CHUNK), 3 * CHUNK), G * 3 * CHUNK), :] = (
                    kq.reshape(G * 3 * CHUNK, CHUNK))
                el_scr[d, hh, pl.ds(grp * G, G)] = jnp.broadcast_to(el[part], (G, 8, LANES))
        return carry

    lax.fori_loop(0, n // G, prepare, 0)
    s_scr[...] = s0_ref[...]

    def scan(t, carry):
        chains = [(hh, d) for hh in range(HG) for d in range(2)]
        chunk = [t if d == 0 else n - 1 - t for _, d in chains]
        states = [s_scr[d, hh] for hh, d in chains]
        from_state = []
        for (hh, d), c, s in zip(chains, chunk, states):
            lhs = wq_scr[d, hh, pl.ds(pl.multiple_of(c * (2 * CHUNK), 2 * CHUNK), 2 * CHUNK), :]
            from_state.append(jnp.dot(lhs, s.astype(BF16), preferred_element_type=F32))
        for (hh, d), c, s, ws_qs in zip(chains, chunk, states, from_state):
            r = pl.ds(pl.multiple_of(c * CHUNK, CHUNK), CHUNK)
            vb = (u_scr[d, hh, r, :] - ws_qs[:CHUNK]).astype(BF16)
            lhs = kq_scr[d, hh, pl.ds(pl.multiple_of(c * (3 * CHUNK), 3 * CHUNK), 3 * CHUNK), :]
            from_new = jnp.dot(lhs, vb, preferred_element_type=F32)
            o_scr[d, hh, r, :] = ws_qs[CHUNK:] + from_new[:CHUNK]
            s_scr[d, hh] = s * el_scr[d, hh, c][0:1, :] + from_new[CHUNK:]
        return carry

    lax.fori_loop(0, n, scan, 0)
    if with_final:
        sfin_ref[...] = s_scr[...]
    for hh in range(HG):
        cols = slice(hh * DK, (hh + 1) * DK)
        o = o_scr[0, hh] + o_scr[1, hh]
        y_ref[:, cols] = (_rms(o, on_ref[...]) * _silu(z_ref[:, cols].astype(F32))).astype(BF16)


def _dn_scan(p, gates, s0, s0_index, onorm_g, *, n_seq, seq_len, row_offset, with_final):
    blk0 = row_offset // seq_len
    n = seq_len // CHUNK
    HG = max(PREP_BATCH // n, HEADS_DN // 2)
    n_grp = HEADS_DN // HG

    def part(k):
        return pl.BlockSpec((seq_len, HG * DK), lambda b, h: (blk0 + b, k * n_grp + h))

    out_specs = [pl.BlockSpec((seq_len, HG * DK), lambda b, h: (b, h))]
    out_shape = [jax.ShapeDtypeStruct((n_seq * seq_len, D), BF16)]
    if with_final:
        out_specs.append(pl.BlockSpec((None, 2, HG, DK, DK), lambda b, h: (b, 0, h, 0, 0)))
        out_shape.append(jax.ShapeDtypeStruct((n_seq, 2, HEADS_DN, DK, DK), F32))
    n = seq_len // CHUNK
    res = pl.pallas_call(
        functools.partial(_dn_scan_kernel, seq_len=seq_len, with_final=with_final, HG=HG),
        grid=(n_seq, n_grp),
        in_specs=[
            part(0), part(1), part(2), part(3),
            pl.BlockSpec((seq_len, LANES), lambda b, h: (blk0 + b, 0)),
            pl.BlockSpec((None,) * (s0.ndim - 4) + (2, HG, DK, DK), s0_index),
            pl.BlockSpec((1, DK), lambda b, h: (0, 0)),
        ],
        out_specs=out_specs,
        out_shape=out_shape,
        scratch_shapes=[
            pltpu.VMEM((2, HG, seq_len, DK), F32),
            pltpu.VMEM((2, HG, 2 * seq_len, DK), BF16),
            pltpu.VMEM((2, HG, 3 * seq_len, CHUNK), BF16),
            pltpu.VMEM((2, HG, n, 8, LANES), F32),
            pltpu.VMEM((2, HG, seq_len, DK), F32),
            pltpu.VMEM((2, HG, DK, DK), F32),
        ],
        compiler_params=pltpu.CompilerParams(
            dimension_semantics=("arbitrary", "arbitrary"), vmem_limit_bytes=VMEM_LIMIT),
        name="deltanet_scan_%d" % seq_len,
    )(p, p, p, p, gates, s0, onorm_g)
    return res


def _rope_tile(x, cos, sin):
    lane = lax.broadcasted_iota(jnp.int32, x.shape, 1)
    width = x.shape[1]
    partner = jnp.where((lane & 31) < 16, pltpu.roll(x, width - 16, 1), pltpu.roll(x, 16, 1))
    return x * cos + partner * sin


def _at_in_kernel(x_ref, sh_ref, sc_ref, gpre_ref, w_ref, cos_ref, sin_ref, q_ref, kv_ref, h_scr):
    i = pl.program_id(0)
    h_scr[...] = _pre(x_ref[...], gpre_ref[...], sh_ref[...], sc_ref[...]).astype(BF16)
    stages = [(s, t) for s in range(N_SEG) for t in _sub_tiles(AT_Q + 2 * AT_KVW)]

    def project(stage):
        s, (c0, cw) = stage
        return jnp.dot(h_scr[s * SEG:(s + 1) * SEG, :], w_ref[:, c0:c0 + cw], preferred_element_type=F32)

    def consume(stage, u, rope):
        s, (c0, cw) = stage
        rows = slice(s * SEG, (s + 1) * SEG)
        if rope and c0 < AT_Q + AT_KVW:
            u = jnp.concatenate([_rope_tile(u[:, t * LANES:(t + 1) * LANES], cos_ref[rows, :], sin_ref[rows, :])
                                 for t in range(cw // LANES)], axis=1)
        if c0 < AT_Q:
            q_ref[rows, c0:c0 + cw] = (u * Q_SCALE).astype(BF16)
        else:
            kv_ref[rows, c0 - AT_Q:c0 - AT_Q + cw] = u

    @pl.when(i >= PROMPT_BLKS)
    def _():
        _pipelined(stages, project, functools.partial(consume, rope=True))

    @pl.when(i < PROMPT_BLKS)
    def _():
        _pipelined(stages, project, functools.partial(consume, rope=False))


def _at_in_layer(x, mod, gains, layer, w_qkv, cos, sin):
    width = AT_Q + 2 * AT_KVW
    row = pl.BlockSpec((ROW_BLK, D), lambda i: (i, 0))
    tab = pl.BlockSpec((ROW_BLK, LANES), lambda i: (0, 0))
    return pl.pallas_call(
        _at_in_kernel,
        grid=(N_BLKS,),
        in_specs=[row, _mod_spec(layer, 0), _mod_spec(layer, 1), _gain_spec(layer, 0),
                  pl.BlockSpec((D, width), lambda i: (0, 0)), tab, tab],
        out_specs=[pl.BlockSpec((ROW_BLK, AT_Q), lambda i: (i, 0)),
                   pl.BlockSpec((ROW_BLK, 2 * AT_KVW), lambda i: (i, 0))],
        out_shape=[jax.ShapeDtypeStruct((N_TOK, AT_Q), BF16), jax.ShapeDtypeStruct((N_TOK, 2 * AT_KVW), F32)],
        scratch_shapes=[pltpu.VMEM((ROW_BLK, D), BF16)],
        compiler_params=pltpu.CompilerParams(
            dimension_semantics=("arbitrary",), vmem_limit_bytes=VMEM_LIMIT),
        name="attention_in",
    )(x, mod, mod, gains, w_qkv, cos, sin)


def _softmax_pv(scores, values, sink):
    m = sink
    for s in scores:
        m = jnp.maximum(m, jnp.max(s, axis=0, keepdims=True))
    denom = jnp.exp2(sink - m)
    acc = None
    for s, v in zip(scores, values):
        p = jnp.exp2(s - m)
        denom = denom + jnp.sum(p, axis=0, keepdims=True)
        pv = _bdot_tn(v, p)
        acc = pv if acc is None else acc + pv
    return acc / denom


def _attend_groups(q_ref, scores_of, values_of, sink_ref):
    rows = q_ref.shape[0]

    def stacked_q(kvh):
        return jnp.concatenate([q_ref[:, (kvh * AT_G + g) * HD:(kvh * AT_G + g + 1) * HD] for g in range(AT_G)], axis=0)

    def stacked_sink(kvh):
        return jnp.concatenate([jnp.broadcast_to(sink_ref[0:1, kvh * AT_G + g:kvh * AT_G + g + 1], (1, rows))
                                for g in range(AT_G)], axis=1)

    outs = []
    pending = scores_of(stacked_q(0), 0)
    for kvh in range(AT_KV):
        upcoming = scores_of(stacked_q(kvh + 1), kvh + 1) if kvh + 1 < AT_KV else None
        o = _softmax_pv(pending, values_of(kvh), stacked_sink(kvh))
        outs.extend(o[:, g * rows:(g + 1) * rows] for g in range(AT_G))
        pending = upcoming
    return jnp.concatenate(outs, axis=0).T.astype(BF16)


def _ctx_attn_kernel(q_ref, kv_ref, sink_ref, y_ref):
    keys = [kv_ref[:, kvh * HD:(kvh + 1) * HD].astype(BF16) for kvh in range(AT_KV)]
    vals = [kv_ref[:, AT_KVW + kvh * HD:AT_KVW + (kvh + 1) * HD].astype(BF16) for kvh in range(AT_KV)]
    y_ref[...] = _attend_groups(q_ref, lambda q, kvh: [_bdot_nt(keys[kvh], q)], lambda kvh: [vals[kvh]], sink_ref)


def _ctx_attention(q, kv, sink):
    nq = L_PROMPT // QBLK
    return pl.pallas_call(
        _ctx_attn_kernel,
        grid=(N_PROMPT, nq),
        in_specs=[
            pl.BlockSpec((QBLK, AT_Q), lambda b, t: (b * nq + t, 0)),
            pl.BlockSpec((L_PROMPT, 2 * AT_KVW), lambda b, t: (b, 0)),
            pl.BlockSpec((1, LANES), lambda b, t: (0, 0)),
        ],
        out_specs=pl.BlockSpec((QBLK, AT_Q), lambda b, t: (b * nq + t, 0)),
        out_shape=jax.ShapeDtypeStruct((TOK_PROMPT, AT_Q), BF16),
        compiler_params=pltpu.CompilerParams(dimension_semantics=("arbitrary", "arbitrary")),
        name="context_attention",
    )(q, kv, sink)


def _lat_attn_kernel(q_ref, kv_ref, ck_ref, cv_ref, sink_ref, y_ref):
    t = pl.program_id(1)
    k0 = pl.multiple_of(jnp.clip(t * QBLK - WINDOW, 0, L_LATENT - BAND), QBLK)
    kpos = k0 + lax.broadcasted_iota(jnp.int32, (BAND, QBLK), 0)
    qpos = t * QBLK + lax.broadcasted_iota(jnp.int32, (BAND, QBLK), 1)
    valid = jnp.concatenate([jnp.abs(qpos - kpos) <= WINDOW] * AT_G, axis=1)
    band = pl.ds(k0, BAND)
    k_loc = [kv_ref[band, kvh * HD:(kvh + 1) * HD].astype(BF16) for kvh in range(AT_KV)]
    v_loc = [kv_ref[band, AT_KVW + kvh * HD:AT_KVW + (kvh + 1) * HD].astype(BF16) for kvh in range(AT_KV)]
    k_ctx = [ck_ref[:, kvh * HD:(kvh + 1) * HD].astype(BF16) for kvh in range(AT_KV)]
    v_ctx = [cv_ref[:, kvh * HD:(kvh + 1) * HD].astype(BF16) for kvh in range(AT_KV)]

    def scores_of(q, kvh):
        return [jnp.where(valid, _bdot_nt(k_loc[kvh], q), -jnp.inf), _bdot_nt(k_ctx[kvh], q)]

    y_ref[...] = _attend_groups(q_ref, scores_of, lambda kvh: [v_loc[kvh], v_ctx[kvh]], sink_ref)


def _lat_attention(q, kv, cache_k, cache_v, sink, attn_layer):
    nq = L_LATENT // QBLK
    q0 = TOK_PROMPT // QBLK
    s0 = TOK_PROMPT // L_LATENT
    cache = pl.BlockSpec((None, None, PAST, AT_KVW), lambda b, t: (b, attn_layer, 0, 0))
    return pl.pallas_call(
        _lat_attn_kernel,
        grid=(N_LATENT, nq),
        in_specs=[
            pl.BlockSpec((QBLK, AT_Q), lambda b, t: (q0 + b * nq + t, 0)),
            pl.BlockSpec((L_LATENT, 2 * AT_KVW), lambda b, t: (s0 + b, 0)),
            cache, cache,
            pl.BlockSpec((1, LANES), lambda b, t: (0, 0)),
        ],
        out_specs=pl.BlockSpec((QBLK, AT_Q), lambda b, t: (b * nq + t, 0)),
        out_shape=jax.ShapeDtypeStruct((N_LATENT * L_LATENT, AT_Q), BF16),
        compiler_params=pltpu.CompilerParams(dimension_semantics=("arbitrary", "arbitrary")),
        name="latent_attention",
    )(q, kv, cache_k, cache_v, sink)


def _rope_tables():
    pos = jnp.arange(L_LATENT)
    half = HD // 2
    inv = jnp.power(ROPE_BASE, -jnp.arange(0, half, 2, dtype=F32) / half)
    ang_row = (pos // GRID_W).astype(F32)[:, None] * inv
    ang_col = (pos % GRID_W).astype(F32)[:, None] * inv
    ang = jnp.concatenate([ang_row, ang_row, ang_col, ang_col], axis=1)
    sign = jnp.tile(jnp.concatenate([-jnp.ones((16,), F32), jnp.ones((16,), F32)]), 2)
    cos = jnp.tile(jnp.cos(ang), (1, 2))
    sin = jnp.tile(jnp.sin(ang) * sign, (1, 2))
    return cos, sin


def _pad_lanes(v):
    v = v.reshape(1, -1).astype(F32)
    return jnp.pad(v, ((0, 0), (0, LANES - v.shape[1])))


def kernel(x_prompt, x_sample, state_delta, cache_k, cache_v, c, c_ctx, w_ada, b_ada, norm_g, dn_w_in, dn_conv_w, dn_a_log, dn_dt_bias, dn_onorm_g, dn_w_out, at_w_qkv, at_sink, at_w_o, ffn_w_up, ffn_conv_w, ffn_conv_b, ffn_w_down):
    x = (x_prompt.reshape(TOK_PROMPT, D), x_sample.reshape(N_TOK - TOK_PROMPT, D))
    cvecs = jnp.concatenate([c_ctx[None, :], c, jnp.zeros((MOD_ROWS - 1 - N_LATENT, D), F32)], axis=0)
    mod = _modulation(cvecs, w_ada, b_ada).reshape(DEPTH, MOD_ROWS, 1, 6 * D)
    gains = norm_g.reshape(DEPTH, 4, 1, D)
    cos, sin = _rope_tables()
    ck = cache_k.reshape(N_LATENT, -1, PAST, AT_KVW)
    cv = cache_v.reshape(N_LATENT, -1, PAST, AT_KVW)
    zero_state = jnp.zeros((2, HEADS_DN, DK, DK), F32)

    states, new_k, new_v = [], [], []
    for layer in range(DEPTH):
        j = layer // 2
        if layer % 2 == 0:
            w_in = dn_w_in[j]
            w_main = w_in[:, :DN_MAIN].astype(BF16)
            w_ab = jnp.pad(w_in[:, DN_MAIN:], ((0, 0), (0, LANES - 4 * HEADS_DN))).astype(BF16)
            conv_w = jnp.pad(dn_conv_w[j], ((0, 0), (0, D)))
            p, gates = _dn_in_layer(x, mod, gains, layer, w_main, conv_w, w_ab,
                                    _pad_lanes(dn_a_log[j]), _pad_lanes(dn_dt_bias[j]))
            onorm = dn_onorm_g[j].reshape(1, DK)
            y_p, s_fin = _dn_scan(p, gates, zero_state, lambda b, h: (0, h, 0, 0), onorm,
                                  n_seq=N_PROMPT, seq_len=L_PROMPT, row_offset=0, with_final=True)
            (y_s,) = _dn_scan(p, gates, state_delta, lambda b, h, j=j: (b, j, 0, h, 0, 0), onorm,
                              n_seq=N_LATENT, seq_len=L_LATENT, row_offset=TOK_PROMPT, with_final=False)
            states.append(s_fin)
            w_o = dn_w_out[j]
        else:
            q, kv = _at_in_layer(x, mod, gains, layer, at_w_qkv[j].astype(BF16), cos, sin)
            sink = _pad_lanes(at_sink[j]) * LOG2E
            y_p = _ctx_attention(q, kv, sink)
            y_s = _lat_attention(q, kv, ck, cv, sink, j)
            new_k.append(kv[:TOK_PROMPT, :AT_KVW].reshape(N_PROMPT, L_PROMPT, AT_KV, HD))
            new_v.append(kv[:TOK_PROMPT, AT_KVW:].reshape(N_PROMPT, L_PROMPT, AT_KV, HD))
            w_o = at_w_o[j]
        x = _out_layer(y_p, y_s, w_o.astype(BF16), x, mod, gains, layer)
        x = _ffn_layer(x, mod, gains, layer, ffn_w_up[layer].astype(BF16), ffn_conv_w[layer],
                       ffn_conv_b[layer].reshape(1, 2 * FFN), ffn_w_down[layer].astype(BF16),
                       split_out=(layer == DEPTH - 1))

    y_prompt = x[0].reshape(N_PROMPT, L_PROMPT, D)
    y_sample = x[1].reshape(N_LATENT, L_LATENT, D)
    return (y_prompt, y_sample, jnp.stack(states, axis=1), jnp.stack(new_k, axis=1), jnp.stack(new_v, axis=1))
```
